```python
import math
import jax, jax.numpy as jnp
from jax import lax
import numpy as np

D_MODEL = 2048
BATCH = 2
SEQ = 16384
DEPTH = 2

N_MEM = 256
MEM_HEADS = 4
MEM_HEAD_DIM = 128
MEM_WIDTH = MEM_HEADS * MEM_HEAD_DIM
MIX_WIDTH = D_MODEL
TOKEN_WIDTH = MIX_WIDTH - MEM_WIDTH

POOL_GROUPS = 4
POOL_WINDOWS = (2, 4, 8, 16)
POOL_GROUP_DIM = TOKEN_WIDTH // POOL_GROUPS

MLA_HEADS = 12
QK_NOPE_DIM = 128
QK_ROPE_DIM = 64
V_HEAD_DIM = 128
Q_LORA_RANK = 512
KV_LORA_RANK = 512
ROPE_THETA = 10000.0
Q_BLOCK = 128
MLA_SCALE = (QK_NOPE_DIM + QK_ROPE_DIM) ** -0.5
MLA_IN_WIDTH = Q_LORA_RANK + KV_LORA_RANK + QK_ROPE_DIM + MEM_WIDTH
MLA_OUT_WIDTH = MLA_HEADS * V_HEAD_DIM + MEM_WIDTH

D_FF = 5632
CONV_WIDTH = 3

N_POOL_LAYERS = (DEPTH + 1) // 2
N_MLA_LAYERS = DEPTH // 2

ALPHA = (2 * DEPTH) ** 0.25
BETA = (8 * DEPTH) ** -0.25

LN_EPS = 1e-5
RMS_EPS = 1e-6

kernel_name = "hybrid_pool_mla_memory_encoder"


def layer_norm(u, g, b):
    uf = u.astype(jnp.float32)
    mu = jnp.mean(uf, axis=-1, keepdims=True)
    var = jnp.mean(jnp.square(uf - mu), axis=-1, keepdims=True)
    y = (uf - mu) * lax.rsqrt(var + LN_EPS)
    return (y * g.astype(jnp.float32) + b.astype(jnp.float32)).astype(u.dtype)


def rms_norm(u, g):
    uf = u.astype(jnp.float32)
    y = uf * lax.rsqrt(jnp.mean(jnp.square(uf), axis=-1, keepdims=True) + RMS_EPS)
    return (y * g.astype(jnp.float32)).astype(u.dtype)


def rotary_tables(positions):
    inv_freq = ROPE_THETA ** (-jnp.arange(0, QK_ROPE_DIM, 2, dtype=jnp.float32) / QK_ROPE_DIM)
    ang = positions.astype(jnp.float32)[..., None] * inv_freq
    return jnp.cos(ang), jnp.sin(ang)


def apply_rotary(u, cos, sin):
    uf = u.astype(jnp.float32)
    half = QK_ROPE_DIM // 2
    u1, u2 = uf[..., :half], uf[..., half:]
    out = jnp.concatenate([u1 * cos - u2 * sin, u2 * cos + u1 * sin], axis=-1)
    return out.astype(u.dtype)


def centred_window_mean(u, w):
    B, S, C = u.shape
    uf = u.astype(jnp.float32)
    cs = jnp.concatenate([jnp.zeros((B, 1, C), jnp.float32), jnp.cumsum(uf, axis=1)], axis=1)
    t = jnp.arange(S)
    lo = jnp.clip(t - w // 2, 0, S)
    hi = jnp.clip(t + w // 2, 0, S)
    total = jnp.take(cs, hi, axis=1) - jnp.take(cs, lo, axis=1)
    count = (hi - lo).astype(jnp.float32)
    return total / count[None, :, None]


def pool_mixer(u, group_w, scale):
    B, S, _ = u.shape
    g = u.reshape(B, S, POOL_GROUPS, POOL_GROUP_DIM)
    pooled = jnp.stack(
        [centred_window_mean(g[:, :, k], POOL_WINDOWS[k]) - g[:, :, k].astype(jnp.float32)
         for k in range(POOL_GROUPS)], axis=2).astype(u.dtype)
    y = jnp.einsum('bsgc,gcd->bsgd', pooled, group_w).reshape(B, S, TOKEN_WIDTH)
    return y * scale


def memory_attention(q, mem, w_kv):
    B, S, _ = q.shape
    kv = mem @ w_kv
    k = kv[..., :MEM_WIDTH].reshape(B, N_MEM, MEM_HEADS, MEM_HEAD_DIM)
    v = kv[..., MEM_WIDTH:].reshape(B, N_MEM, MEM_HEADS, MEM_HEAD_DIM)
    qh = q.reshape(B, S, MEM_HEADS, MEM_HEAD_DIM)
    s = jnp.einsum('bshd,bmhd->bhsm', qh, k).astype(jnp.float32) * (MEM_HEAD_DIM ** -0.5)
    p = jax.nn.softmax(s, axis=-1).astype(v.dtype)
    return jnp.einsum('bhsm,bmhd->bshd', p, v).reshape(B, S, MEM_WIDTH)


def mla_attention(q_nope, q_rope, k_nope, k_rope, v):
    B, S, H, _ = q_nope.shape
    nblk = S // Q_BLOCK

    def to_blocks(t):
        return jnp.moveaxis(t.reshape(B, nblk, Q_BLOCK, *t.shape[2:]), 1, 0)

    def attend(blk):
        qn, qr = blk
        s = (jnp.einsum('bqhd,bkhd->bhqk', qn, k_nope)
             + jnp.einsum('bqhr,bkr->bhqk', qr, k_rope))
        p = jax.nn.softmax(s.astype(jnp.float32) * MLA_SCALE, axis=-1).astype(v.dtype)
        return jnp.einsum('bhqk,bkhd->bqhd', p, v)

    o = lax.map(attend, (to_blocks(q_nope), to_blocks(q_rope)))
    return jnp.moveaxis(o, 0, 1).reshape(B, S, H * V_HEAD_DIM)


def pool_layer_mix(x, mem, w_in, group_w, scale, w_out, w_mem_kv):
    z = x @ w_in
    u_pool, q_mem = z[..., :TOKEN_WIDTH], z[..., TOKEN_WIDTH:]
    y_tok = pool_mixer(u_pool, group_w, scale)
    y_mem = memory_attention(q_mem, mem, w_mem_kv)
    return jnp.concatenate([y_tok, y_mem], axis=-1) @ w_out


def mla_layer_mix(x, mem, cos, sin, w_in, q_norm_g, kv_norm_g, w_uq, w_uk, w_uv, w_out, w_mem_kv):
    B, S, _ = x.shape
    z = x @ w_in
    o1 = Q_LORA_RANK
    o2 = o1 + KV_LORA_RANK
    o3 = o2 + QK_ROPE_DIM
    q_lat, kv_lat, k_rope, q_mem = z[..., :o1], z[..., o1:o2], z[..., o2:o3], z[..., o3:]
    q = (rms_norm(q_lat, q_norm_g) @ w_uq).reshape(B, S, MLA_HEADS, QK_NOPE_DIM + QK_ROPE_DIM)
    q_nope = q[..., :QK_NOPE_DIM]
    q_rope = apply_rotary(q[..., QK_NOPE_DIM:], cos[:, :, None, :], sin[:, :, None, :])
    c_kv = rms_norm(kv_lat, kv_norm_g)
    k_nope = (c_kv @ w_uk).reshape(B, S, MLA_HEADS, QK_NOPE_DIM)
    v = (c_kv @ w_uv).reshape(B, S, MLA_HEADS, V_HEAD_DIM)
    k_rope = apply_rotary(k_rope, cos, sin)
    y_tok = mla_attention(q_nope, q_rope, k_nope, k_rope, v)
    y_mem = memory_attention(q_mem, mem, w_mem_kv)
    return jnp.concatenate([y_tok, y_mem], axis=-1) @ w_out


def conv_ffn(x, w_up, conv_w, conv_b, w_down):
    h = x @ w_up
    c = h.shape[-1]
    h = lax.conv_general_dilated(
        h, conv_w[:, None, :], window_strides=(1,), padding=((CONV_WIDTH // 2, CONV_WIDTH // 2),),
        dimension_numbers=('NWC', 'WIO', 'NWC'), feature_group_count=c) + conv_b
    gate, val = h[..., :D_FF], h[..., D_FF:]
    return (jax.nn.silu(gate) * val) @ w_down


def setup_inputs(seed: int = 0) -> dict:
    key = jax.random.key(seed)
    ks = jax.random.split(key, 32)
    f32 = jnp.float32

    def nrm(k, shape, scale):
        return jax.random.normal(k, shape, f32) * scale

    x = nrm(ks[0], (BATCH, SEQ, D_MODEL), 1.0)
    mem = nrm(ks[1], (BATCH, N_MEM, D_MODEL), 1.0)
    offset = jax.random.randint(ks[2], (BATCH, 1), 0, 4096, dtype=jnp.int32)
    positions = offset + jnp.arange(SEQ, dtype=jnp.int32)[None, :]

    pool_w_in = nrm(ks[3], (N_POOL_LAYERS, D_MODEL, MIX_WIDTH), D_MODEL ** -0.5)
    pool_group_w = nrm(ks[4], (N_POOL_LAYERS, POOL_GROUPS, POOL_GROUP_DIM, POOL_GROUP_DIM), POOL_GROUP_DIM ** -0.5)
    pool_scale = 1.0 + nrm(ks[5], (N_POOL_LAYERS, TOKEN_WIDTH), 0.1)
    pool_w_out = nrm(ks[6], (N_POOL_LAYERS, MIX_WIDTH, D_MODEL), BETA * MIX_WIDTH ** -0.5)

    mla_w_in = nrm(ks[7], (N_MLA_LAYERS, D_MODEL, MLA_IN_WIDTH), D_MODEL ** -0.5)
    mla_q_norm_g = 1.0 + nrm(ks[8], (N_MLA_LAYERS, Q_LORA_RANK), 0.02)
    mla_kv_norm_g = 1.0 + nrm(ks[9], (N_MLA_LAYERS, KV_LORA_RANK), 0.02)
    mla_w_uq = nrm(ks[10], (N_MLA_LAYERS, Q_LORA_RANK, MLA_HEADS * (QK_NOPE_DIM + QK_ROPE_DIM)), Q_LORA_RANK ** -0.5)
    mla_w_uk = nrm(ks[11], (N_MLA_LAYERS, KV_LORA_RANK, MLA_HEADS * QK_NOPE_DIM), KV_LORA_RANK ** -0.5)
    mla_w_uv = nrm(ks[12], (N_MLA_LAYERS, KV_LORA_RANK, MLA_HEADS * V_HEAD_DIM), BETA * KV_LORA_RANK ** -0.5)
    mla_w_out = nrm(ks[13], (N_MLA_LAYERS, MLA_OUT_WIDTH, D_MODEL), BETA * MLA_OUT_WIDTH ** -0.5)

    mem_k = nrm(ks[14], (DEPTH, D_MODEL, MEM_WIDTH), D_MODEL ** -0.5)
    mem_v = nrm(ks[15], (DEPTH, D_MODEL, MEM_WIDTH), BETA * D_MODEL ** -0.5)
    mem_w_kv = jnp.concatenate([mem_k, mem_v], axis=-1)

    ln1_g = 1.0 + nrm(ks[16], (DEPTH, D_MODEL), 0.02)
    ln1_b = nrm(ks[17], (DEPTH, D_MODEL), 0.02)
    ln2_g = 1.0 + nrm(ks[18], (DEPTH, D_MODEL), 0.02)
    ln2_b = nrm(ks[19], (DEPTH, D_MODEL), 0.02)

    ffn_w_up = nrm(ks[20], (DEPTH, D_MODEL, 2 * D_FF), BETA * D_MODEL ** -0.5)
    ffn_conv_w = nrm(ks[21], (DEPTH, CONV_WIDTH, 2 * D_FF), CONV_WIDTH ** -0.5)
    ffn_conv_b = nrm(ks[22], (DEPTH, 2 * D_FF), 0.02)
    ffn_w_down = nrm(ks[23], (DEPTH, D_FF, D_MODEL), BETA * D_FF ** -0.5)

    return {
        "x": x, "mem": mem, "positions": positions,
        "pool_w_in": pool_w_in, "pool_group_w": pool_group_w, "pool_scale": pool_scale, "pool_w_out": pool_w_out,
        "mla_w_in": mla_w_in, "mla_q_norm_g": mla_q_norm_g, "mla_kv_norm_g": mla_kv_norm_g,
        "mla_w_uq": mla_w_uq, "mla_w_uk": mla_w_uk, "mla_w_uv": mla_w_uv, "mla_w_out": mla_w_out,
        "mem_w_kv": mem_w_kv,
        "ln1_g": ln1_g, "ln1_b": ln1_b, "ln2_g": ln2_g, "ln2_b": ln2_b,
        "ffn_w_up": ffn_w_up, "ffn_conv_w": ffn_conv_w, "ffn_conv_b": ffn_conv_b, "ffn_w_down": ffn_w_down,
    }


def reference(x, mem, positions,
              pool_w_in, pool_group_w, pool_scale, pool_w_out,
              mla_w_in, mla_q_norm_g, mla_kv_norm_g, mla_w_uq, mla_w_uk, mla_w_uv, mla_w_out,
              mem_w_kv,
              ln1_g, ln1_b, ln2_g, ln2_b,
              ffn_w_up, ffn_conv_w, ffn_conv_b, ffn_w_down):
    cos, sin = rotary_tables(positions)
    for i in range(DEPTH):
        j = i // 2
        if i % 2 == 0:
            h = pool_layer_mix(x, mem, pool_w_in[j], pool_group_w[j], pool_scale[j], pool_w_out[j], mem_w_kv[i])
        else:
            h = mla_layer_mix(x, mem, cos, sin, mla_w_in[j], mla_q_norm_g[j], mla_kv_norm_g[j],
                              mla_w_uq[j], mla_w_uk[j], mla_w_uv[j], mla_w_out[j], mem_w_kv[i])
        x = layer_norm(ALPHA * x + h, ln1_g[i], ln1_b[i])
        f = conv_ffn(x, ffn_w_up[i], ffn_conv_w[i], ffn_conv_b[i], ffn_w_down[i])
        x = layer_norm(ALPHA * x + f, ln2_g[i], ln2_b[i])
    return x
```

```python
import functools

import jax
import jax.numpy as jnp
from jax import lax
from jax.experimental import pallas as pl
from jax.experimental.pallas import tpu as pltpu

F32 = jnp.float32
BF16 = jnp.bfloat16

N_MEM = 256
MEM_HEADS = 4
MEM_HEAD_DIM = 128
MEM_WIDTH = MEM_HEADS * MEM_HEAD_DIM
POOL_WINDOWS = (2, 4, 8, 16)
MLA_HEADS = 12
QK_NOPE_DIM = 128
QK_ROPE_DIM = 64
QK_DIM = QK_NOPE_DIM + QK_ROPE_DIM
V_HEAD_DIM = 128
Q_LORA_RANK = 512
KV_LORA_RANK = 512
ROPE_THETA = 10000.0
MLA_SCALE = QK_DIM ** -0.5
MEM_SCALE = MEM_HEAD_DIM ** -0.5
LN_EPS = 1e-5
RMS_EPS = 1e-6

LANES = 128
BF16_SUBLANES = 16
HALO = BF16_SUBLANES
VMEM_LIMIT_BYTES = 56 * 1024 * 1024


def _params(*semantics):
    return pltpu.CompilerParams(dimension_semantics=semantics, vmem_limit_bytes=VMEM_LIMIT_BYTES)


def _tile(n, pref):
    t = min(n, pref)
    assert n % t == 0, (n, t)
    return t


def _const_spec(shape):
    nd = len(shape)
    return pl.BlockSpec(shape, lambda *_: (0,) * nd)


def _mm_kernel(x_ref, w_ref, o_ref):
    o_ref[...] = jnp.dot(x_ref[...].astype(BF16), w_ref[...],
                         preferred_element_type=F32).astype(o_ref.dtype)


def _matmul(x, w, out_dtype, tm=1024, tn=1024):
    m, k = x.shape
    n = w.shape[1]
    tm, tn = _tile(m, tm), _tile(n, tn)
    return pl.pallas_call(
        _mm_kernel,
        grid=(m // tm, n // tn),
        in_specs=[pl.BlockSpec((tm, k), lambda i, j: (i, 0)),
                  pl.BlockSpec((k, tn), lambda i, j: (0, j))],
        out_specs=pl.BlockSpec((tm, tn), lambda i, j: (i, j)),
        out_shape=jax.ShapeDtypeStruct((m, n), out_dtype),
        compiler_params=_params("parallel", "arbitrary"),
        name="matmul",
    )(x, w)


def _rope_kernel(pos_ref, invf_ref, cos_ref, sin_ref):
    ang = pos_ref[...].astype(F32) * invf_ref[...]
    cos_ref[...] = jnp.cos(ang)
    sin_ref[...] = jnp.sin(ang)


def _rope_tables(positions, tm=1024):
    t = positions.size
    tm = _tile(t, tm)
    inv_freq = ROPE_THETA ** (-jnp.arange(0, QK_ROPE_DIM, 2, dtype=F32) / QK_ROPE_DIM)
    invf = jnp.tile(inv_freq, LANES // inv_freq.shape[0]).reshape(1, LANES)
    out = jax.ShapeDtypeStruct((t, LANES), F32)
    return pl.pallas_call(
        _rope_kernel,
        grid=(t // tm,),
        in_specs=[pl.BlockSpec((tm, 1), lambda i: (i, 0)), _const_spec((1, LANES))],
        out_specs=[pl.BlockSpec((tm, LANES), lambda i: (i, 0))] * 2,
        out_shape=[out, out],
        compiler_params=_params("parallel"),
        name="rope_tables",
    )(positions.reshape(t, 1), invf)


def _pool_kernel(zm_ref, zp_ref, zn_ref, gw_ref, sc_ref, o_ref, *, tm, n_s, seq):
    si = pl.program_id(0) % n_s
    first, last = si == 0, si == n_s - 1
    n = tm + 2 * HALO
    gdim = gw_ref.shape[1]
    pos = si * tm + lax.broadcasted_iota(jnp.int32, (tm, 1), 0)
    for k, w in enumerate(POOL_WINDOWS):
        cols = slice(k * gdim, (k + 1) * gdim)
        main = zm_ref[:, cols].astype(F32)
        nxt = jnp.where(last, 0.0, zn_ref[:, cols].astype(F32))
        prv = jnp.where(first, 0.0, zp_ref[:, cols].astype(F32))
        u = jnp.concatenate([main, nxt, prv], axis=0)
        acc = u + pltpu.roll(u, 1, 0)
        half = 1
        while 2 * half < w:
            acc = pltpu.roll(acc, half, 0) + pltpu.roll(acc, n - half, 0)
            half *= 2
        cnt = (jnp.minimum(pos + w // 2, seq) - jnp.maximum(pos - w // 2, 0)).astype(F32)
        pooled = acc[:tm] / cnt - main
        y = jnp.dot(pooled.astype(BF16), gw_ref[k], preferred_element_type=F32)
        o_ref[:, cols] = (y * sc_ref[:, cols]).astype(o_ref.dtype)


def _pool_mix(z, group_w, scale, seq, tm=512):
    t = z.shape[0]
    groups, gdim, _ = group_w.shape
    width = groups * gdim
    tm = _tile(seq, tm)
    n_s = seq // tm
    hb = tm // HALO
    return pl.pallas_call(
        functools.partial(_pool_kernel, tm=tm, n_s=n_s, seq=seq),
        grid=(t // tm,),
        in_specs=[pl.BlockSpec((tm, width), lambda i: (i, 0)),
                  pl.BlockSpec((HALO, width), lambda i: (jnp.maximum(i * hb - 1, 0), 0)),
                  pl.BlockSpec((HALO, width), lambda i: (jnp.minimum((i + 1) * hb, t // HALO - 1), 0)),
                  _const_spec(group_w.shape), _const_spec((1, width))],
        out_specs=pl.BlockSpec((tm, width), lambda i: (i, 0)),
        out_shape=jax.ShapeDtypeStruct((t, width), BF16),
        compiler_params=_params("parallel"),
        name="pool_mix",
    )(z, z, z, group_w, scale.reshape(1, width))


def _mem_attn_kernel(q_ref, kv_ref, o_ref):
    for h in range(MEM_HEADS):
        cols = slice(h * MEM_HEAD_DIM, (h + 1) * MEM_HEAD_DIM)
        vcols = slice(MEM_WIDTH + h * MEM_HEAD_DIM, MEM_WIDTH + (h + 1) * MEM_HEAD_DIM)
        s = lax.dot_general(q_ref[:, cols], kv_ref[:, cols], (((1,), (1,)), ((), ())),
                            preferred_element_type=F32) * MEM_SCALE
        m = jnp.max(s, axis=-1, keepdims=True)
        p = jnp.exp(s - m)
        l = jnp.sum(p, axis=-1, keepdims=True)
        o = jnp.dot(p.astype(BF16), kv_ref[:, vcols], preferred_element_type=F32)
        o_ref[:, cols] = (o / l).astype(o_ref.dtype)


def _mem_attn(q_src, q_col_block, kv, seq, tm=512):
    t = q_src.shape[0]
    tm = _tile(seq, tm)
    n_s = seq // tm
    return pl.pallas_call(
        _mem_attn_kernel,
        grid=(t // tm,),
        in_specs=[pl.BlockSpec((tm, MEM_WIDTH), lambda i: (i, q_col_block)),
                  pl.BlockSpec((N_MEM, 2 * MEM_WIDTH), lambda i: (i // n_s, 0))],
        out_specs=pl.BlockSpec((tm, MEM_WIDTH), lambda i: (i, 0)),
        out_shape=jax.ShapeDtypeStruct((t, MEM_WIDTH), BF16),
        compiler_params=_params("parallel"),
        name="mem_attn",
    )(q_src, kv)


def _layer_norm(u, g, b):
    mu = jnp.mean(u, axis=-1, keepdims=True)
    d = u - mu
    var = jnp.mean(d * d, axis=-1, keepdims=True)
    return d * lax.rsqrt(var + LN_EPS) * g + b


def _outproj_kernel(a_ref, b_ref, wa_ref, wb_ref, x_ref, g_ref, beta_ref, of_ref, ob_ref, *, alpha):
    h = jnp.dot(a_ref[...], wa_ref[...], preferred_element_type=F32)
    h = h + jnp.dot(b_ref[...], wb_ref[...], preferred_element_type=F32)
    y = _layer_norm(alpha * x_ref[...] + h, g_ref[...], beta_ref[...])
    of_ref[...] = y
    ob_ref[...] = y.astype(BF16)


def _outproj_ln(a, b, w_a, w_b, x, g, beta, alpha, tm=512):
    t, d = x.shape
    tm = _tile(t, tm)
    ka, kb = a.shape[1], b.shape[1]
    row = lambda width: pl.BlockSpec((tm, width), lambda i: (i, 0))
    return pl.pallas_call(
        functools.partial(_outproj_kernel, alpha=alpha),
        grid=(t // tm,),
        in_specs=[row(ka), row(kb), _const_spec((ka, d)), _const_spec((kb, d)), row(d),
                  _const_spec((1, d)), _const_spec((1, d))],
        out_specs=[row(d), row(d)],
        out_shape=[jax.ShapeDtypeStruct((t, d), F32), jax.ShapeDtypeStruct((t, d), BF16)],
        compiler_params=_params("parallel"),
        name="outproj_ln",
    )(a, b, w_a, w_b, x, g.reshape(1, d), beta.reshape(1, d))


def _ffn_kernel(xm_ref, xp_ref, xn_ref, xres_ref, wg_ref, wv_ref, cwg_ref, cwv_ref, cbg_ref, cbv_ref,
                wd_ref, g_ref, beta_ref, of_ref, ob_ref, xext_ref, *, tm, n_s, alpha):
    j = pl.program_id(1)
    n = tm + 2 * HALO

    @pl.when(j == 0)
    def _():
        si = pl.program_id(0) % n_s
        xext_ref[0:tm, :] = xm_ref[...]
        xext_ref[tm:tm + HALO, :] = jnp.where(si == n_s - 1, jnp.zeros_like(xn_ref), xn_ref[...])
        xext_ref[tm + HALO:n, :] = jnp.where(si == 0, jnp.zeros_like(xp_ref), xp_ref[...])

    xe = xext_ref[...]

    def conv_branch(w_ref, cw_ref, cb_ref):
        h = jnp.dot(xe, w_ref[...], preferred_element_type=F32)
        c = (pltpu.roll(h, 1, 0) * cw_ref[0:1, :] + h * cw_ref[1:2, :]
             + pltpu.roll(h, n - 1, 0) * cw_ref[2:3, :])
        return c[:tm] + cb_ref[...]

    gate = conv_branch(wg_ref, cwg_ref, cbg_ref)
    val = conv_branch(wv_ref, cwv_ref, cbv_ref)
    act = (gate * jax.nn.sigmoid(gate) * val).astype(BF16)
    f = jnp.dot(act, wd_ref[...], preferred_element_type=F32)

    @pl.when(j == 0)
    def _():
        of_ref[...] = f

    @pl.when(j > 0)
    def _():
        of_ref[...] += f

    @pl.when(j == pl.num_programs(1) - 1)
    def _():
        y = _layer_norm(alpha * xres_ref[...] + of_ref[...], g_ref[...], beta_ref[...])
        of_ref[...] = y
        ob_ref[...] = y.astype(BF16)


def _conv_ffn_ln(x_f32, x_bf16, w_up, conv_w, conv_b, w_down, g, beta, alpha, seq, tm=512, tn=512):
    t, d = x_f32.shape
    d_ff = w_down.shape[0]
    tm, tn = _tile(seq, tm), _tile(d_ff, tn)
    n_s, n_j = seq // tm, d_ff // tn
    hb = tm // HALO
    row = lambda i, j: (i, 0)
    gate_col = lambda i, j: (0, j)
    val_col = lambda i, j: (0, j + n_j)
    return pl.pallas_call(
        functools.partial(_ffn_kernel, tm=tm, n_s=n_s, alpha=alpha),
        grid=(t // tm, n_j),
        in_specs=[pl.BlockSpec((tm, d), row),
                  pl.BlockSpec((HALO, d), lambda i, j: (jnp.maximum(i * hb - 1, 0), 0)),
                  pl.BlockSpec((HALO, d), lambda i, j: (jnp.minimum((i + 1) * hb, t // HALO - 1), 0)),
                  pl.BlockSpec((tm, d), row),
                  pl.BlockSpec((d, tn), gate_col), pl.BlockSpec((d, tn), val_col),
                  pl.BlockSpec((3, tn), gate_col), pl.BlockSpec((3, tn), val_col),
                  pl.BlockSpec((1, tn), gate_col), pl.BlockSpec((1, tn), val_col),
                  pl.BlockSpec((tn, d), lambda i, j: (j, 0)),
                  _const_spec((1, d)), _const_spec((1, d))],
        out_specs=[pl.BlockSpec((tm, d), row), pl.BlockSpec((tm, d), row)],
        out_shape=[jax.ShapeDtypeStruct((t, d), F32), jax.ShapeDtypeStruct((t, d), BF16)],
        scratch_shapes=[pltpu.VMEM((tm + 2 * HALO, d), BF16)],
        compiler_params=_params("parallel", "arbitrary"),
        name="conv_ffn_ln",
    )(x_bf16, x_bf16, x_bf16, x_f32, w_up, w_up, conv_w, conv_w,
      conv_b.reshape(1, -1), conv_b.reshape(1, -1), w_down, g.reshape(1, d), beta.reshape(1, d))


def _rms_norm(u, g):
    return u * lax.rsqrt(jnp.mean(u * u, axis=-1, keepdims=True) + RMS_EPS) * g


def _mla_proj_kernel(x_ref, win_ref, gq_ref, gkv_ref, wqn_ref, wqr_ref, wqrot_ref, wuk_ref, wuv_ref,
                     cos_ref, sin_ref, q_ref, k_ref, v_ref, qmem_ref):
    z = jnp.dot(x_ref[...], win_ref[...], preferred_element_type=F32)
    o1 = Q_LORA_RANK
    o2 = o1 + KV_LORA_RANK
    o3 = o2 + MEM_WIDTH
    o4 = o3 + QK_ROPE_DIM
    qmem_ref[...] = z[:, o2:o3].astype(qmem_ref.dtype)
    qn = _rms_norm(z[:, :o1], gq_ref[...]).astype(BF16)
    ckv = _rms_norm(z[:, o1:o2], gkv_ref[...]).astype(BF16)
    cos, sin = cos_ref[...], sin_ref[...]

    k_rope = (z[:, o3:o4] * cos[:, :QK_ROPE_DIM] + z[:, o4:o4 + QK_ROPE_DIM] * sin[:, :QK_ROPE_DIM])
    k_rope = k_rope.astype(k_ref.dtype)

    q_nope = jnp.dot(qn, wqn_ref[...], preferred_element_type=F32) * MLA_SCALE
    q_ra = jnp.dot(qn, wqr_ref[...], preferred_element_type=F32)
    q_rb = jnp.dot(qn, wqrot_ref[...], preferred_element_type=F32)
    k_nope = jnp.dot(ckv, wuk_ref[...], preferred_element_type=F32)
    v = jnp.dot(ckv, wuv_ref[...], preferred_element_type=F32)

    heads_per_group = LANES // QK_ROPE_DIM
    for g in range(MLA_HEADS // heads_per_group):
        lanes = slice(g * LANES, (g + 1) * LANES)
        rope = ((q_ra[:, lanes] * cos + q_rb[:, lanes] * sin) * MLA_SCALE).astype(q_ref.dtype)
        for r in range(heads_per_group):
            h = g * heads_per_group + r
            q_ref[0, h, :, QK_NOPE_DIM:QK_DIM] = rope[:, r * QK_ROPE_DIM:(r + 1) * QK_ROPE_DIM]
    for h in range(MLA_HEADS):
        q_ref[0, h, :, 0:QK_NOPE_DIM] = q_nope[:, h * QK_NOPE_DIM:(h + 1) * QK_NOPE_DIM].astype(q_ref.dtype)
        k_ref[0, h, :, 0:QK_NOPE_DIM] = k_nope[:, h * QK_NOPE_DIM:(h + 1) * QK_NOPE_DIM].astype(k_ref.dtype)
        k_ref[0, h, :, QK_NOPE_DIM:QK_DIM] = k_rope
        v_ref[0, h, :, :] = v[:, h * V_HEAD_DIM:(h + 1) * V_HEAD_DIM].astype(v_ref.dtype)


def _mla_proj(x_bf16, w_in, gq, gkv, w_qn, w_qr, w_qrot, w_uk, w_uv, cos, sin, batch, seq, tm=512):
    t, d = x_bf16.shape
    tm = _tile(seq, tm)
    n_s = seq // tm
    row = lambda width: pl.BlockSpec((tm, width), lambda i: (i, 0))
    head_blk = lambda width: pl.BlockSpec((1, MLA_HEADS, tm, width), lambda i: (i // n_s, 0, i % n_s, 0))
    consts = [w_in, gq, gkv, w_qn, w_qr, w_qrot, w_uk, w_uv]
    return pl.pallas_call(
        _mla_proj_kernel,
        grid=(t // tm,),
        in_specs=[row(d)] + [_const_spec(c.shape) for c in consts] + [row(LANES), row(LANES)],
        out_specs=[head_blk(QK_DIM), head_blk(QK_DIM), head_blk(V_HEAD_DIM), row(MEM_WIDTH)],
        out_shape=[jax.ShapeDtypeStruct((batch, MLA_HEADS, seq, QK_DIM), BF16),
                   jax.ShapeDtypeStruct((batch, MLA_HEADS, seq, QK_DIM), BF16),
                   jax.ShapeDtypeStruct((batch, MLA_HEADS, seq, V_HEAD_DIM), BF16),
                   jax.ShapeDtypeStruct((t, MEM_WIDTH), BF16)],
        compiler_params=_params("parallel"),
        name="mla_proj",
    )(x_bf16, *consts, cos, sin)


def _flash_kernel(q_ref, k_ref, v_ref, o_ref, m_ref, l_ref, acc_ref):
    kv = pl.program_id(3)

    @pl.when(kv == 0)
    def _():
        m_ref[...] = jnp.full_like(m_ref, -jnp.inf)
        l_ref[...] = jnp.zeros_like(l_ref)
        acc_ref[...] = jnp.zeros_like(acc_ref)

    s = lax.dot_general(q_ref[0, 0], k_ref[0, 0], (((1,), (1,)), ((), ())),
                        preferred_element_type=F32)
    m_prev = m_ref[...]
    m_new = jnp.maximum(m_prev, jnp.max(s, axis=-1, keepdims=True))
    a = jnp.exp(m_prev - m_new)
    p = jnp.exp(s - m_new)
    l_ref[...] = a * l_ref[...] + jnp.sum(p, axis=-1, keepdims=True)
    acc_ref[...] = a * acc_ref[...] + jnp.dot(p.astype(BF16), v_ref[0, 0], preferred_element_type=F32)
    m_ref[...] = m_new

    @pl.when(kv == pl.num_programs(3) - 1)
    def _():
        o_ref[0] = (acc_ref[...] / l_ref[...]).astype(o_ref.dtype)


def _flash_attention(q, k, v, tq=1024, tk=512):
    batch, heads, seq, dk = q.shape
    dv = v.shape[-1]
    tq, tk = _tile(seq, tq), _tile(seq, tk)
    return pl.pallas_call(
        _flash_kernel,
        grid=(batch, heads, seq // tq, seq // tk),
        in_specs=[pl.BlockSpec((1, 1, tq, dk), lambda b, h, i, j: (b, h, i, 0)),
                  pl.BlockSpec((1, 1, tk, dk), lambda b, h, i, j: (b, h, j, 0)),
                  pl.BlockSpec((1, 1, tk, dv), lambda b, h, i, j: (b, h, j, 0))],
        out_specs=pl.BlockSpec((1, tq, dv), lambda b, h, i, j: (b, i, h)),
        out_shape=jax.ShapeDtypeStruct((batch, seq, heads * dv), BF16),
        scratch_shapes=[pltpu.VMEM((tq, 1), F32), pltpu.VMEM((tq, 1), F32), pltpu.VMEM((tq, dv), F32)],
        compiler_params=_params("parallel", "parallel", "parallel", "arbitrary"),
        name="flash_attention",
    )(q, k, v)


def _rotate_half_cols(w):
    half = w.shape[-1] // 2
    return jnp.concatenate([-w[..., half:], w[..., :half]], axis=-1)


def _mla_weights(w_in, w_uq):
    o1 = Q_LORA_RANK
    o2 = o1 + KV_LORA_RANK
    o3 = o2 + QK_ROPE_DIM
    k_rope_w = w_in[:, o2:o3]
    w_in_r = jnp.concatenate([w_in[:, :o2], w_in[:, o3:], k_rope_w, _rotate_half_cols(k_rope_w)], axis=1)
    rank = w_uq.shape[0]
    w3 = w_uq.reshape(rank, MLA_HEADS, QK_DIM)
    w_qn = w3[:, :, :QK_NOPE_DIM].reshape(rank, MLA_HEADS * QK_NOPE_DIM)
    w_qr = w3[:, :, QK_NOPE_DIM:]
    w_qrot = _rotate_half_cols(w_qr).reshape(rank, MLA_HEADS * QK_ROPE_DIM)
    w_qr = w_qr.reshape(rank, MLA_HEADS * QK_ROPE_DIM)
    return w_in_r.astype(BF16), w_qn.astype(BF16), w_qr.astype(BF16), w_qrot.astype(BF16)


def kernel(x, mem, positions, pool_w_in, pool_group_w, pool_scale, pool_w_out, mla_w_in, mla_q_norm_g,
           mla_kv_norm_g, mla_w_uq, mla_w_uk, mla_w_uv, mla_w_out, mem_w_kv, ln1_g, ln1_b, ln2_g, ln2_b,
           ffn_w_up, ffn_conv_w, ffn_conv_b, ffn_w_down):
    batch, seq, d = x.shape
    depth = ln1_g.shape[0]
    alpha = (2 * depth) ** 0.25
    t = batch * seq
    token_width = pool_w_in.shape[-1] - MEM_WIDTH

    xf = x.reshape(t, d)
    xb = None
    mem2 = mem.reshape(batch * N_MEM, d)
    cos = sin = None
    if depth > 1:
        cos, sin = _rope_tables(positions)

    for i in range(depth):
        j = i // 2
        kv_mem = _matmul(mem2, mem_w_kv[i].astype(BF16), BF16)
        if i % 2 == 0:
            z = _matmul(xf if xb is None else xb, pool_w_in[j].astype(BF16), BF16)
            y_tok = _pool_mix(z, pool_group_w[j].astype(BF16), pool_scale[j], seq)
            y_mem = _mem_attn(z, token_width // MEM_WIDTH, kv_mem, seq)
            w_out = pool_w_out[j].astype(BF16)
        else:
            w_in_r, w_qn, w_qr, w_qrot = _mla_weights(mla_w_in[j], mla_w_uq[j])
            q, k, v, q_mem = _mla_proj(
                xb, w_in_r, mla_q_norm_g[j].reshape(1, -1), mla_kv_norm_g[j].reshape(1, -1),
                w_qn, w_qr, w_qrot, mla_w_uk[j].astype(BF16), mla_w_uv[j].astype(BF16),
                cos, sin, batch, seq)
            y_tok = _flash_attention(q, k, v).reshape(t, MLA_HEADS * V_HEAD_DIM)
            y_mem = _mem_attn(q_mem, 0, kv_mem, seq)
            w_out = mla_w_out[j].astype(BF16)
        split = y_tok.shape[1]
        xf, xb = _outproj_ln(y_tok, y_mem, w_out[:split], w_out[split:], xf, ln1_g[i], ln1_b[i], alpha)
        xf, xb = _conv_ffn_ln(xf, xb, ffn_w_up[i].astype(BF16), ffn_conv_w[i], ffn_conv_b[i],
                              ffn_w_down[i].astype(BF16), ln2_g[i], ln2_b[i], alpha, seq)
    return xf.reshape(batch, seq, d)
```

```python
import functools

import jax
import jax.numpy as jnp
from jax import lax
from jax.experimental import pallas as pl
from jax.experimental.pallas import tpu as pltpu

F32 = jnp.float32
BF16 = jnp.bfloat16

N_MEM = 256
MEM_HEADS = 4
MEM_HEAD_DIM = 128
MEM_WIDTH = MEM_HEADS * MEM_HEAD_DIM
POOL_WINDOWS = (2, 4, 8, 16)
MLA_HEADS = 12
QK_NOPE_DIM = 128
QK_ROPE_DIM = 64
QK_DIM = QK_NOPE_DIM + QK_ROPE_DIM
V_HEAD_DIM = 128
Q_LORA_RANK = 512
KV_LORA_RANK = 512
ROPE_THETA = 10000.0
ROPE_FREQS = QK_ROPE_DIM // 2
MLA_SCALE = QK_DIM ** -0.5
LOG2_E = 1.4426950408889634
Q_SCALE = MLA_SCALE * LOG2_E
MEM_SCALE = MEM_HEAD_DIM ** -0.5
LN_EPS = 1e-5
RMS_EPS = 1e-6

LANES = 128
BF16_SUBLANES = 16
HALO = BF16_SUBLANES
VMEM_LIMIT_BYTES = 56 * 1024 * 1024


def _params(*semantics):
    return pltpu.CompilerParams(dimension_semantics=semantics, vmem_limit_bytes=VMEM_LIMIT_BYTES)


def _tile(n, pref):
    t = min(n, pref)
    assert n % t == 0, (n, t)
    return t


def _const_spec(shape):
    nd = len(shape)
    return pl.BlockSpec(shape, lambda *_: (0,) * nd)


def _mm_kernel(x_ref, w_ref, o_ref):
    o_ref[...] = jnp.dot(x_ref[...].astype(BF16), w_ref[...],
                         preferred_element_type=F32).astype(o_ref.dtype)


def _matmul(x, w, out_dtype, tm=1024, tn=1024):
    m, k = x.shape
    n = w.shape[1]
    tm, tn = _tile(m, tm), _tile(n, tn)
    return pl.pallas_call(
        _mm_kernel,
        grid=(m // tm, n // tn),
        in_specs=[pl.BlockSpec((tm, k), lambda i, j: (i, 0)),
                  pl.BlockSpec((k, tn), lambda i, j: (0, j))],
        out_specs=pl.BlockSpec((tm, tn), lambda i, j: (i, j)),
        out_shape=jax.ShapeDtypeStruct((m, n), out_dtype),
        compiler_params=_params("parallel", "arbitrary"),
        name="matmul",
    )(x, w)


def _rope_kernel(pos_ref, invf_ref, tab_ref):
    ang = invf_ref[...] * pos_ref[...].astype(F32)
    tab_ref[0:ROPE_FREQS, :] = jnp.cos(ang)
    tab_ref[ROPE_FREQS:, :] = jnp.sin(ang)


def _rope_table(positions, tm=2048):
    t = positions.size
    tm = _tile(t, tm)
    inv_freq = ROPE_THETA ** (-jnp.arange(0, QK_ROPE_DIM, 2, dtype=F32) / QK_ROPE_DIM)
    return pl.pallas_call(
        _rope_kernel,
        grid=(t // tm,),
        in_specs=[pl.BlockSpec((1, tm), lambda i: (0, i)), _const_spec((ROPE_FREQS, 1))],
        out_specs=pl.BlockSpec((2 * ROPE_FREQS, tm), lambda i: (0, i)),
        out_shape=jax.ShapeDtypeStruct((2 * ROPE_FREQS, t), F32),
        compiler_params=_params("parallel"),
        name="rope_table",
    )(positions.reshape(1, t), inv_freq.reshape(ROPE_FREQS, 1))


def _pool_kernel(zm_ref, zp_ref, zn_ref, gw_ref, sc_ref, o_ref, *, tm, n_s, seq):
    si = pl.program_id(0) % n_s
    first, last = si == 0, si == n_s - 1
    n = tm + 2 * HALO
    gdim = gw_ref.shape[1]
    pos = si * tm + lax.broadcasted_iota(jnp.int32, (tm, 1), 0)
    for k, w in enumerate(POOL_WINDOWS):
        cols = slice(k * gdim, (k + 1) * gdim)
        main = zm_ref[:, cols].astype(F32)
        nxt = jnp.where(last, 0.0, zn_ref[:, cols].astype(F32))
        prv = jnp.where(first, 0.0, zp_ref[:, cols].astype(F32))
        u = jnp.concatenate([main, nxt, prv], axis=0)
        acc = u + pltpu.roll(u, 1, 0)
        half = 1
        while 2 * half < w:
            acc = pltpu.roll(acc, half, 0) + pltpu.roll(acc, n - half, 0)
            half *= 2
        cnt = (jnp.minimum(pos + w // 2, seq) - jnp.maximum(pos - w // 2, 0)).astype(F32)
        pooled = acc[:tm] / cnt - main
        y = jnp.dot(pooled.astype(BF16), gw_ref[k], preferred_element_type=F32)
        o_ref[:, cols] = (y * sc_ref[:, cols]).astype(o_ref.dtype)


def _pool_mix(z, group_w, scale, seq, tm=512):
    t = z.shape[0]
    groups, gdim, _ = group_w.shape
    width = groups * gdim
    tm = _tile(seq, tm)
    n_s = seq // tm
    hb = tm // HALO
    return pl.pallas_call(
        functools.partial(_pool_kernel, tm=tm, n_s=n_s, seq=seq),
        grid=(t // tm,),
        in_specs=[pl.BlockSpec((tm, width), lambda i: (i, 0)),
                  pl.BlockSpec((HALO, width), lambda i: (jnp.maximum(i * hb - 1, 0), 0)),
                  pl.BlockSpec((HALO, width), lambda i: (jnp.minimum((i + 1) * hb, t // HALO - 1), 0)),
                  _const_spec(group_w.shape), _const_spec((1, width))],
        out_specs=pl.BlockSpec((tm, width), lambda i: (i, 0)),
        out_shape=jax.ShapeDtypeStruct((t, width), BF16),
        compiler_params=_params("parallel"),
        name="pool_mix",
    )(z, z, z, group_w, scale.reshape(1, width))


def _mem_attn_kernel(q_ref, kv_ref, o_ref):
    for h in range(MEM_HEADS):
        cols = slice(h * MEM_HEAD_DIM, (h + 1) * MEM_HEAD_DIM)
        vcols = slice(MEM_WIDTH + h * MEM_HEAD_DIM, MEM_WIDTH + (h + 1) * MEM_HEAD_DIM)
        s = lax.dot_general(q_ref[:, cols], kv_ref[:, cols], (((1,), (1,)), ((), ())),
                            preferred_element_type=F32) * MEM_SCALE
        m = jnp.max(s, axis=-1, keepdims=True)
        p = jnp.exp(s - m)
        l = jnp.sum(p, axis=-1, keepdims=True)
        o = jnp.dot(p.astype(BF16), kv_ref[:, vcols], preferred_element_type=F32)
        o_ref[:, cols] = (o / l).astype(o_ref.dtype)


def _mem_attn(q_src, q_col_block, kv, seq, tm=512):
    t = q_src.shape[0]
    tm = _tile(seq, tm)
    n_s = seq // tm
    return pl.pallas_call(
        _mem_attn_kernel,
        grid=(t // tm,),
        in_specs=[pl.BlockSpec((tm, MEM_WIDTH), lambda i: (i, q_col_block)),
                  pl.BlockSpec((N_MEM, 2 * MEM_WIDTH), lambda i: (i // n_s, 0))],
        out_specs=pl.BlockSpec((tm, MEM_WIDTH), lambda i: (i, 0)),
        out_shape=jax.ShapeDtypeStruct((t, MEM_WIDTH), BF16),
        compiler_params=_params("parallel"),
        name="mem_attn",
    )(q_src, kv)


def _layer_norm(u, g, b):
    mu = jnp.mean(u, axis=-1, keepdims=True)
    d = u - mu
    var = jnp.mean(d * d, axis=-1, keepdims=True)
    return d * lax.rsqrt(var + LN_EPS) * g + b


def _outproj_kernel(a_ref, b_ref, wa_ref, wb_ref, x_ref, g_ref, beta_ref, of_ref, ob_ref, *, alpha):
    h = jnp.dot(a_ref[...], wa_ref[...], preferred_element_type=F32)
    h = h + jnp.dot(b_ref[...], wb_ref[...], preferred_element_type=F32)
    y = _layer_norm(alpha * x_ref[...] + h, g_ref[...], beta_ref[...])
    of_ref[...] = y
    ob_ref[...] = y.astype(BF16)


def _outproj_ln(a, b, w_a, w_b, x, g, beta, alpha, tm=512):
    t, d = x.shape
    tm = _tile(t, tm)
    ka, kb = a.shape[1], b.shape[1]
    row = lambda width: pl.BlockSpec((tm, width), lambda i: (i, 0))
    return pl.pallas_call(
        functools.partial(_outproj_kernel, alpha=alpha),
        grid=(t // tm,),
        in_specs=[row(ka), row(kb), _const_spec((ka, d)), _const_spec((kb, d)), row(d),
                  _const_spec((1, d)), _const_spec((1, d))],
        out_specs=[row(d), row(d)],
        out_shape=[jax.ShapeDtypeStruct((t, d), F32), jax.ShapeDtypeStruct((t, d), BF16)],
        compiler_params=_params("parallel"),
        name="outproj_ln",
    )(a, b, w_a, w_b, x, g.reshape(1, d), beta.reshape(1, d))


def _ffn_kernel(xm_ref, xp_ref, xn_ref, xres_ref, wg_ref, wv_ref, cwg_ref, cwv_ref, cbg_ref, cbv_ref,
                wd_ref, g_ref, beta_ref, of_ref, ob_ref, xext_ref, *, tm, n_s, alpha):
    j = pl.program_id(1)
    n = tm + 2 * HALO

    @pl.when(j == 0)
    def _():
        si = pl.program_id(0) % n_s
        xext_ref[0:tm, :] = xm_ref[...]
        xext_ref[tm:tm + HALO, :] = jnp.where(si == n_s - 1, jnp.zeros_like(xn_ref), xn_ref[...])
        xext_ref[tm + HALO:n, :] = jnp.where(si == 0, jnp.zeros_like(xp_ref), xp_ref[...])

    xe = xext_ref[...]

    def conv_branch(w_ref, cw_ref, cb_ref):
        h = jnp.dot(xe, w_ref[...], preferred_element_type=F32)
        c = (pltpu.roll(h, 1, 0) * cw_ref[0:1, :] + h * cw_ref[1:2, :]
             + pltpu.roll(h, n - 1, 0) * cw_ref[2:3, :])
        return c[:tm] + cb_ref[...]

    gate = conv_branch(wg_ref, cwg_ref, cbg_ref)
    val = conv_branch(wv_ref, cwv_ref, cbv_ref)
    act = (gate * jax.nn.sigmoid(gate) * val).astype(BF16)
    f = jnp.dot(act, wd_ref[...], preferred_element_type=F32)

    @pl.when(j == 0)
    def _():
        of_ref[...] = f

    @pl.when(j > 0)
    def _():
        of_ref[...] += f

    @pl.when(j == pl.num_programs(1) - 1)
    def _():
        y = _layer_norm(alpha * xres_ref[...] + of_ref[...], g_ref[...], beta_ref[...])
        of_ref[...] = y
        ob_ref[...] = y.astype(BF16)


def _conv_ffn_ln(x_f32, x_bf16, w_up, conv_w, conv_b, w_down, g, beta, alpha, seq, tm=512, tn=512):
    t, d = x_f32.shape
    d_ff = w_down.shape[0]
    tm, tn = _tile(seq, tm), _tile(d_ff, tn)
    n_s, n_j = seq // tm, d_ff // tn
    hb = tm // HALO
    row = lambda i, j: (i, 0)
    gate_col = lambda i, j: (0, j)
    val_col = lambda i, j: (0, j + n_j)
    return pl.pallas_call(
        functools.partial(_ffn_kernel, tm=tm, n_s=n_s, alpha=alpha),
        grid=(t // tm, n_j),
        in_specs=[pl.BlockSpec((tm, d), row),
                  pl.BlockSpec((HALO, d), lambda i, j: (jnp.maximum(i * hb - 1, 0), 0)),
                  pl.BlockSpec((HALO, d), lambda i, j: (jnp.minimum((i + 1) * hb, t // HALO - 1), 0)),
                  pl.BlockSpec((tm, d), row),
                  pl.BlockSpec((d, tn), gate_col), pl.BlockSpec((d, tn), val_col),
                  pl.BlockSpec((3, tn), gate_col), pl.BlockSpec((3, tn), val_col),
                  pl.BlockSpec((1, tn), gate_col), pl.BlockSpec((1, tn), val_col),
                  pl.BlockSpec((tn, d), lambda i, j: (j, 0)),
                  _const_spec((1, d)), _const_spec((1, d))],
        out_specs=[pl.BlockSpec((tm, d), row), pl.BlockSpec((tm, d), row)],
        out_shape=[jax.ShapeDtypeStruct((t, d), F32), jax.ShapeDtypeStruct((t, d), BF16)],
        scratch_shapes=[pltpu.VMEM((tm + 2 * HALO, d), BF16)],
        compiler_params=_params("parallel", "arbitrary"),
        name="conv_ffn_ln",
    )(x_bf16, x_bf16, x_bf16, x_f32, w_up, w_up, conv_w, conv_w,
      conv_b.reshape(1, -1), conv_b.reshape(1, -1), w_down, g.reshape(1, d), beta.reshape(1, d))


def _rms_norm(u, g):
    return u * lax.rsqrt(jnp.mean(u * u, axis=-1, keepdims=True) + RMS_EPS) * g


def _mla_proj_kernel(x_ref, win_ref, gq_ref, gkv_ref, wqn_t_ref, wqr_t_ref, wqrot_t_ref, wuk_ref, wuv_t_ref,
                     tab_ref, qt_ref, k_ref, vt_ref, qmem_ref):
    nt = (((1,), (1,)), ((), ()))
    z = jnp.dot(x_ref[...], win_ref[...], preferred_element_type=F32)
    o1 = Q_LORA_RANK
    o2 = o1 + KV_LORA_RANK
    o3 = o2 + MEM_WIDTH
    o4 = o3 + QK_ROPE_DIM
    qmem_ref[...] = z[:, o2:o3].astype(qmem_ref.dtype)
    qn = _rms_norm(z[:, :o1], gq_ref[...]).astype(BF16)
    ckv = _rms_norm(z[:, o1:o2], gkv_ref[...]).astype(BF16)

    cos_t, sin_t = tab_ref[0:ROPE_FREQS, :], tab_ref[ROPE_FREQS:, :]
    rope_rows = jnp.concatenate([cos_t, cos_t, sin_t, sin_t], axis=0)
    cos2_t, sin2_t = rope_rows[:QK_ROPE_DIM], rope_rows[QK_ROPE_DIM:]
    rope_cols = rope_rows.T

    k_rope = (z[:, o3:o4] * rope_cols[:, :QK_ROPE_DIM] + z[:, o4:o4 + QK_ROPE_DIM] * rope_cols[:, QK_ROPE_DIM:])
    k_rope = k_rope.astype(k_ref.dtype)
    k_nope = jnp.dot(ckv, wuk_ref[...], preferred_element_type=F32)

    qt_nope = lax.dot_general(wqn_t_ref[...], qn, nt, preferred_element_type=F32) * Q_SCALE
    qt_ra = lax.dot_general(wqr_t_ref[...], qn, nt, preferred_element_type=F32)
    qt_rb = lax.dot_general(wqrot_t_ref[...], qn, nt, preferred_element_type=F32)
    vt = lax.dot_general(wuv_t_ref[...], ckv, nt, preferred_element_type=F32)

    for h in range(MLA_HEADS):
        nope = slice(h * QK_NOPE_DIM, (h + 1) * QK_NOPE_DIM)
        rope = slice(h * QK_ROPE_DIM, (h + 1) * QK_ROPE_DIM)
        qt_rope = (qt_ra[rope] * cos2_t + qt_rb[rope] * sin2_t) * Q_SCALE
        qt_ref[0, h, 0:QK_NOPE_DIM, :] = qt_nope[nope].astype(qt_ref.dtype)
        qt_ref[0, h, QK_NOPE_DIM:QK_DIM, :] = qt_rope.astype(qt_ref.dtype)
        k_ref[0, h, :, 0:QK_NOPE_DIM] = k_nope[:, nope].astype(k_ref.dtype)
        k_ref[0, h, :, QK_NOPE_DIM:QK_DIM] = k_rope
        vt_ref[0, h, :, :] = vt[h * V_HEAD_DIM:(h + 1) * V_HEAD_DIM].astype(vt_ref.dtype)


def _mla_proj(x_bf16, w_in, gq, gkv, w_qn_t, w_qr_t, w_qrot_t, w_uk, w_uv_t, rope_tab, batch, seq, tm=512):
    t, d = x_bf16.shape
    tm = _tile(seq, tm)
    n_s = seq // tm
    row = lambda width: pl.BlockSpec((tm, width), lambda i: (i, 0))
    rows_blk = lambda width: pl.BlockSpec((1, MLA_HEADS, tm, width), lambda i: (i // n_s, 0, i % n_s, 0))
    cols_blk = lambda height: pl.BlockSpec((1, MLA_HEADS, height, tm), lambda i: (i // n_s, 0, 0, i % n_s))
    consts = [w_in, gq, gkv, w_qn_t, w_qr_t, w_qrot_t, w_uk, w_uv_t]
    return pl.pallas_call(
        _mla_proj_kernel,
        grid=(t // tm,),
        in_specs=([row(d)] + [_const_spec(c.shape) for c in consts]
                  + [pl.BlockSpec((2 * ROPE_FREQS, tm), lambda i: (0, i))]),
        out_specs=[cols_blk(QK_DIM), rows_blk(QK_DIM), cols_blk(V_HEAD_DIM), row(MEM_WIDTH)],
        out_shape=[jax.ShapeDtypeStruct((batch, MLA_HEADS, QK_DIM, seq), BF16),
                   jax.ShapeDtypeStruct((batch, MLA_HEADS, seq, QK_DIM), BF16),
                   jax.ShapeDtypeStruct((batch, MLA_HEADS, V_HEAD_DIM, seq), BF16),
                   jax.ShapeDtypeStruct((t, MEM_WIDTH), BF16)],
        compiler_params=_params("parallel"),
        name="mla_proj",
    )(x_bf16, *consts, rope_tab)


def _flash_kernel(qt_ref, k_ref, vt_ref, o_ref, s0_ref, s1_ref, p0_ref, p1_ref, cm0_ref, cm1_ref,
                  a0_ref, a1_ref, m_ref, l_ref, acc_ref, *, tk, n_chunks):
    qt = qt_ref[0, 0]
    s_refs, p_refs = (s0_ref, s1_ref), (p0_ref, p1_ref)
    cm_refs, a_refs = (cm0_ref, cm1_ref), (a0_ref, a1_ref)

    def scores(j, slot):
        off = pl.multiple_of(j * tk, tk)
        s = jnp.dot(k_ref[0, 0, pl.ds(off, tk), :], qt, preferred_element_type=F32)
        s_refs[slot][...] = s
        cm_refs[slot][...] = jnp.max(s, axis=0, keepdims=True)

    def softmax(slot):
        m_prev = m_ref[...]
        m_new = jnp.maximum(m_prev, cm_refs[slot][...])
        a = jnp.exp2(m_prev - m_new)
        p = jnp.exp2(s_refs[slot][...] - m_new)
        l_ref[...] = a * l_ref[...] + jnp.sum(p, axis=0, keepdims=True)
        m_ref[...] = m_new
        a_refs[slot][...] = a
        p_refs[slot][...] = p.astype(BF16)

    def values(j, slot):
        off = pl.multiple_of(j * tk, tk)
        pv = jnp.dot(vt_ref[0, 0, :, pl.ds(off, tk)], p_refs[slot][...], preferred_element_type=F32)
        acc_ref[...] = a_refs[slot][...] * acc_ref[...] + pv

    m_ref[...] = jnp.full_like(m_ref, -jnp.inf)
    l_ref[...] = jnp.zeros_like(l_ref)
    acc_ref[...] = jnp.zeros_like(acc_ref)
    scores(0, 0)
    scores(1, 1)
    softmax(0)

    def two_steps(i, carry):
        j = 2 * i + 1
        scores(j + 1, 0)
        softmax(1)
        values(j - 1, 0)
        scores(j + 2, 1)
        softmax(0)
        values(j, 1)
        return carry

    n_pairs = (n_chunks - 2) // 2
    lax.fori_loop(0, n_pairs, two_steps, 0, unroll=3 if n_pairs % 3 == 0 else 1)
    softmax(1)
    values(n_chunks - 2, 0)
    values(n_chunks - 1, 1)
    o_ref[0] = (acc_ref[...] / l_ref[...]).T.astype(o_ref.dtype)


def _flash_attention(qt, k, vt, tq=512, tk=512):
    batch, heads, dk, seq = qt.shape
    dv = vt.shape[2]
    tq, tk = _tile(seq, tq), _tile(seq, tk)
    n_chunks = seq // tk
    assert n_chunks >= 2 and n_chunks % 2 == 0, n_chunks
    stat = pltpu.VMEM((1, tq), F32)
    return pl.pallas_call(
        functools.partial(_flash_kernel, tk=tk, n_chunks=n_chunks),
        grid=(batch, heads, seq // tq),
        in_specs=[pl.BlockSpec((1, 1, dk, tq), lambda b, h, i: (b, h, 0, i)),
                  pl.BlockSpec((1, 1, seq, dk), lambda b, h, i: (b, h, 0, 0)),
                  pl.BlockSpec((1, 1, dv, seq), lambda b, h, i: (b, h, 0, 0))],
        out_specs=pl.BlockSpec((1, tq, dv), lambda b, h, i: (b, i, h)),
        out_shape=jax.ShapeDtypeStruct((batch, seq, heads * dv), BF16),
        scratch_shapes=[pltpu.VMEM((tk, tq), F32), pltpu.VMEM((tk, tq), F32),
                        pltpu.VMEM((tk, tq), BF16), pltpu.VMEM((tk, tq), BF16),
                        stat, stat, stat, stat, stat, stat, pltpu.VMEM((dv, tq), F32)],
        compiler_params=_params("parallel", "parallel", "arbitrary"),
        name="flash_attention",
    )(qt, k, vt)


def _rotate_half_cols(w):
    half = w.shape[-1] // 2
    return jnp.concatenate([-w[..., half:], w[..., :half]], axis=-1)


def _mla_weights(w_in, w_uq):
    o1 = Q_LORA_RANK
    o2 = o1 + KV_LORA_RANK
    o3 = o2 + QK_ROPE_DIM
    k_rope_w = w_in[:, o2:o3]
    w_in_r = jnp.concatenate([w_in[:, :o2], w_in[:, o3:], k_rope_w, _rotate_half_cols(k_rope_w)], axis=1)
    rank = w_uq.shape[0]
    w3 = w_uq.reshape(rank, MLA_HEADS, QK_DIM)
    w_qn = w3[:, :, :QK_NOPE_DIM].reshape(rank, MLA_HEADS * QK_NOPE_DIM)
    w_qr = w3[:, :, QK_NOPE_DIM:]
    w_qrot = _rotate_half_cols(w_qr).reshape(rank, MLA_HEADS * QK_ROPE_DIM)
    w_qr = w_qr.reshape(rank, MLA_HEADS * QK_ROPE_DIM)
    return w_in_r.astype(BF16), w_qn.T.astype(BF16), w_qr.T.astype(BF16), w_qrot.T.astype(BF16)


def kernel(x, mem, positions, pool_w_in, pool_group_w, pool_scale, pool_w_out, mla_w_in, mla_q_norm_g,
           mla_kv_norm_g, mla_w_uq, mla_w_uk, mla_w_uv, mla_w_out, mem_w_kv, ln1_g, ln1_b, ln2_g, ln2_b,
           ffn_w_up, ffn_conv_w, ffn_conv_b, ffn_w_down):
    batch, seq, d = x.shape
    depth = ln1_g.shape[0]
    alpha = (2 * depth) ** 0.25
    t = batch * seq
    token_width = pool_w_in.shape[-1] - MEM_WIDTH

    xf = x.reshape(t, d)
    xb = None
    mem2 = mem.reshape(batch * N_MEM, d)
    rope_tab = _rope_table(positions) if depth > 1 else None

    for i in range(depth):
        j = i // 2
        kv_mem = _matmul(mem2, mem_w_kv[i].astype(BF16), BF16)
        if i % 2 == 0:
            z = _matmul(xf if xb is None else xb, pool_w_in[j].astype(BF16), BF16)
            y_tok = _pool_mix(z, pool_group_w[j].astype(BF16), pool_scale[j], seq)
            y_mem = _mem_attn(z, token_width // MEM_WIDTH, kv_mem, seq)
            w_out = pool_w_out[j].astype(BF16)
        else:
            w_in_r, w_qn_t, w_qr_t, w_qrot_t = _mla_weights(mla_w_in[j], mla_w_uq[j])
            qt, k, vt, q_mem = _mla_proj(
                xb, w_in_r, mla_q_norm_g[j].reshape(1, -1), mla_kv_norm_g[j].reshape(1, -1),
                w_qn_t, w_qr_t, w_qrot_t, mla_w_uk[j].astype(BF16), mla_w_uv[j].T.astype(BF16),
                rope_tab, batch, seq)
            y_tok = _flash_attention(qt, k, vt).reshape(t, MLA_HEADS * V_HEAD_DIM)
            y_mem = _mem_attn(q_mem, 0, kv_mem, seq)
            w_out = mla_w_out[j].astype(BF16)
        split = y_tok.shape[1]
        xf, xb = _outproj_ln(y_tok, y_mem, w_out[:split], w_out[split:], xf, ln1_g[i], ln1_b[i], alpha)
        xf, xb = _conv_ffn_ln(xf, xb, ffn_w_up[i].astype(BF16), ffn_conv_w[i], ffn_conv_b[i],
                              ffn_w_down[i].astype(BF16), ln2_g[i], ln2_b[i], alpha, seq)
    return xf.reshape(batch, seq, d)
```

```python
import functools

import jax
import jax.numpy as jnp
from jax import lax
from jax.experimental import pallas as pl
from jax.experimental.pallas import tpu as pltpu

F32 = jnp.float32
BF16 = jnp.bfloat16

N_MEM = 256
MEM_HEADS = 4
MEM_HEAD_DIM = 128
MEM_WIDTH = MEM_HEADS * MEM_HEAD_DIM
POOL_WINDOWS = (2, 4, 8, 16)
MLA_HEADS = 12
QK_NOPE_DIM = 128
QK_ROPE_DIM = 64
QK_DIM = QK_NOPE_DIM + QK_ROPE_DIM
V_HEAD_DIM = 128
Q_LORA_RANK = 512
KV_LORA_RANK = 512
ROPE_THETA = 10000.0
ROPE_FREQS = QK_ROPE_DIM // 2
MLA_SCALE = QK_DIM ** -0.5
LOG2_E = 1.4426950408889634
Q_SCALE = MLA_SCALE * LOG2_E
MEM_SCALE = MEM_HEAD_DIM ** -0.5
LN_EPS = 1e-5
RMS_EPS = 1e-6

LANES = 128
BF16_SUBLANES = 16
HALO = BF16_SUBLANES
ONES_ROWS = BF16_SUBLANES
VMEM_LIMIT_BYTES = 56 * 1024 * 1024


def _params(*semantics):
    return pltpu.CompilerParams(dimension_semantics=semantics, vmem_limit_bytes=VMEM_LIMIT_BYTES)


def _tile(n, pref):
    t = min(n, pref)
    assert n % t == 0, (n, t)
    return t


def _const_spec(shape):
    nd = len(shape)
    return pl.BlockSpec(shape, lambda *_: (0,) * nd)


def _mm_kernel(x_ref, w_ref, o_ref):
    o_ref[...] = jnp.dot(x_ref[...].astype(BF16), w_ref[...],
                         preferred_element_type=F32).astype(o_ref.dtype)


def _matmul(x, w, out_dtype, tm=1024, tn=1024):
    m, k = x.shape
    n = w.shape[1]
    tm, tn = _tile(m, tm), _tile(n, tn)
    return pl.pallas_call(
        _mm_kernel,
        grid=(m // tm, n // tn),
        in_specs=[pl.BlockSpec((tm, k), lambda i, j: (i, 0)),
                  pl.BlockSpec((k, tn), lambda i, j: (0, j))],
        out_specs=pl.BlockSpec((tm, tn), lambda i, j: (i, j)),
        out_shape=jax.ShapeDtypeStruct((m, n), out_dtype),
        compiler_params=_params("parallel", "arbitrary"),
        name="matmul",
    )(x, w)


def _rope_kernel(pos_ref, invf_ref, tab_ref):
    ang = invf_ref[...] * pos_ref[...].astype(F32)
    tab_ref[0:ROPE_FREQS, :] = jnp.cos(ang)
    tab_ref[ROPE_FREQS:, :] = jnp.sin(ang)


def _rope_table(positions, tm=2048):
    t = positions.size
    tm = _tile(t, tm)
    inv_freq = ROPE_THETA ** (-jnp.arange(0, QK_ROPE_DIM, 2, dtype=F32) / QK_ROPE_DIM)
    return pl.pallas_call(
        _rope_kernel,
        grid=(t // tm,),
        in_specs=[pl.BlockSpec((1, tm), lambda i: (0, i)), _const_spec((ROPE_FREQS, 1))],
        out_specs=pl.BlockSpec((2 * ROPE_FREQS, tm), lambda i: (0, i)),
        out_shape=jax.ShapeDtypeStruct((2 * ROPE_FREQS, t), F32),
        compiler_params=_params("parallel"),
        name="rope_table",
    )(positions.reshape(1, t), inv_freq.reshape(ROPE_FREQS, 1))


def _pool_kernel(zm_ref, zp_ref, zn_ref, gw_ref, sc_ref, o_ref, *, tm, n_s, seq):
    si = pl.program_id(0) % n_s
    first, last = si == 0, si == n_s - 1
    n = tm + 2 * HALO
    gdim = gw_ref.shape[1]
    pos = si * tm + lax.broadcasted_iota(jnp.int32, (tm, 1), 0)
    for k, w in enumerate(POOL_WINDOWS):
        cols = slice(k * gdim, (k + 1) * gdim)
        main = zm_ref[:, cols].astype(F32)
        nxt = jnp.where(last, 0.0, zn_ref[:, cols].astype(F32))
        prv = jnp.where(first, 0.0, zp_ref[:, cols].astype(F32))
        u = jnp.concatenate([main, nxt, prv], axis=0)
        acc = u + pltpu.roll(u, 1, 0)
        half = 1
        while 2 * half < w:
            acc = pltpu.roll(acc, half, 0) + pltpu.roll(acc, n - half, 0)
            half *= 2
        cnt = (jnp.minimum(pos + w // 2, seq) - jnp.maximum(pos - w // 2, 0)).astype(F32)
        pooled = acc[:tm] / cnt - main
        y = jnp.dot(pooled.astype(BF16), gw_ref[k], preferred_element_type=F32)
        o_ref[:, cols] = (y * sc_ref[:, cols]).astype(o_ref.dtype)


def _pool_mix(z, group_w, scale, seq, tm=512):
    t = z.shape[0]
    groups, gdim, _ = group_w.shape
    width = groups * gdim
    tm = _tile(seq, tm)
    n_s = seq // tm
    hb = tm // HALO
    return pl.pallas_call(
        functools.partial(_pool_kernel, tm=tm, n_s=n_s, seq=seq),
        grid=(t // tm,),
        in_specs=[pl.BlockSpec((tm, width), lambda i: (i, 0)),
                  pl.BlockSpec((HALO, width), lambda i: (jnp.maximum(i * hb - 1, 0), 0)),
                  pl.BlockSpec((HALO, width), lambda i: (jnp.minimum((i + 1) * hb, t // HALO - 1), 0)),
                  _const_spec(group_w.shape), _const_spec((1, width))],
        out_specs=pl.BlockSpec((tm, width), lambda i: (i, 0)),
        out_shape=jax.ShapeDtypeStruct((t, width), BF16),
        compiler_params=_params("parallel"),
        name="pool_mix",
    )(z, z, z, group_w, scale.reshape(1, width))


def _mem_attn_kernel(q_ref, kv_ref, o_ref):
    for h in range(MEM_HEADS):
        cols = slice(h * MEM_HEAD_DIM, (h + 1) * MEM_HEAD_DIM)
        vcols = slice(MEM_WIDTH + h * MEM_HEAD_DIM, MEM_WIDTH + (h + 1) * MEM_HEAD_DIM)
        s = lax.dot_general(q_ref[:, cols], kv_ref[:, cols], (((1,), (1,)), ((), ())),
                            preferred_element_type=F32) * MEM_SCALE
        m = jnp.max(s, axis=-1, keepdims=True)
        p = jnp.exp(s - m)
        l = jnp.sum(p, axis=-1, keepdims=True)
        o = jnp.dot(p.astype(BF16), kv_ref[:, vcols], preferred_element_type=F32)
        o_ref[:, cols] = (o / l).astype(o_ref.dtype)


def _mem_attn(q_src, q_col_block, kv, seq, tm=512):
    t = q_src.shape[0]
    tm = _tile(seq, tm)
    n_s = seq // tm
    return pl.pallas_call(
        _mem_attn_kernel,
        grid=(t // tm,),
        in_specs=[pl.BlockSpec((tm, MEM_WIDTH), lambda i: (i, q_col_block)),
                  pl.BlockSpec((N_MEM, 2 * MEM_WIDTH), lambda i: (i // n_s, 0))],
        out_specs=pl.BlockSpec((tm, MEM_WIDTH), lambda i: (i, 0)),
        out_shape=jax.ShapeDtypeStruct((t, MEM_WIDTH), BF16),
        compiler_params=_params("parallel"),
        name="mem_attn",
    )(q_src, kv)


def _layer_norm(u, g, b):
    mu = jnp.mean(u, axis=-1, keepdims=True)
    d = u - mu
    var = jnp.mean(d * d, axis=-1, keepdims=True)
    return d * lax.rsqrt(var + LN_EPS) * g + b


def _outproj_kernel(a_ref, b_ref, wa_ref, wb_ref, x_ref, g_ref, beta_ref, of_ref, ob_ref, *, alpha):
    h = jnp.dot(a_ref[...], wa_ref[...], preferred_element_type=F32)
    h = h + jnp.dot(b_ref[...], wb_ref[...], preferred_element_type=F32)
    y = _layer_norm(alpha * x_ref[...] + h, g_ref[...], beta_ref[...])
    of_ref[...] = y
    ob_ref[...] = y.astype(BF16)


def _outproj_ln(a, b, w_a, w_b, x, g, beta, alpha, tm=512):
    t, d = x.shape
    tm = _tile(t, tm)
    ka, kb = a.shape[1], b.shape[1]
    row = lambda width: pl.BlockSpec((tm, width), lambda i: (i, 0))
    return pl.pallas_call(
        functools.partial(_outproj_kernel, alpha=alpha),
        grid=(t // tm,),
        in_specs=[row(ka), row(kb), _const_spec((ka, d)), _const_spec((kb, d)), row(d),
                  _const_spec((1, d)), _const_spec((1, d))],
        out_specs=[row(d), row(d)],
        out_shape=[jax.ShapeDtypeStruct((t, d), F32), jax.ShapeDtypeStruct((t, d), BF16)],
        compiler_params=_params("parallel"),
        name="outproj_ln",
    )(a, b, w_a, w_b, x, g.reshape(1, d), beta.reshape(1, d))


def _ffn_kernel(xm_ref, xp_ref, xn_ref, xres_ref, wg_ref, wv_ref, cwg_ref, cwv_ref, cbg_ref, cbv_ref,
                wd_ref, g_ref, beta_ref, of_ref, ob_ref, xext_ref, hg_ref, hv_ref, *, tm, n_s, n_j, alpha):
    s = pl.program_id(0)
    last = pl.num_programs(0) - 1
    up_item = jnp.minimum(s, last - 1)
    dn_item = jnp.maximum(s - 1, 0)
    jb = dn_item % n_j
    n = tm + 2 * HALO

    @pl.when(s == 0)
    def _():
        hg_ref[...] = jnp.zeros_like(hg_ref)
        hv_ref[...] = jnp.zeros_like(hv_ref)

    @pl.when(up_item % n_j == 0)
    def _():
        si = (up_item // n_j) % n_s
        xext_ref[0:tm, :] = xm_ref[...]
        xext_ref[tm:tm + HALO, :] = jnp.where(si == n_s - 1, jnp.zeros_like(xn_ref), xn_ref[...])
        xext_ref[tm + HALO:n, :] = jnp.where(si == 0, jnp.zeros_like(xp_ref), xp_ref[...])

    @pl.when(jb == 0)
    def _():
        of_ref[...] = jnp.zeros_like(of_ref)

    def conv_branch(h, cw_ref, cb_ref):
        c = (pltpu.roll(h, 1, 0) * cw_ref[0:1, :] + h * cw_ref[1:2, :]
             + pltpu.roll(h, n - 1, 0) * cw_ref[2:3, :])
        return c[:tm] + cb_ref[...]

    h_gate, h_val = hg_ref[...], hv_ref[...]
    xe = xext_ref[...]
    hg_ref[...] = jnp.dot(xe, wg_ref[...], preferred_element_type=F32)
    hv_ref[...] = jnp.dot(xe, wv_ref[...], preferred_element_type=F32)

    gate = conv_branch(h_gate, cwg_ref, cbg_ref)
    val = conv_branch(h_val, cwv_ref, cbv_ref)
    act = (gate * jax.nn.sigmoid(gate) * val).astype(BF16)
    of_ref[...] += jnp.dot(act, wd_ref[...], preferred_element_type=F32)

    @pl.when((jb == n_j - 1) & (s > 0))
    def _():
        y = _layer_norm(alpha * xres_ref[...] + of_ref[...], g_ref[...], beta_ref[...])
        of_ref[...] = y
        ob_ref[...] = y.astype(BF16)


def _conv_ffn_ln(x_f32, x_bf16, w_up, conv_w, conv_b, w_down, g, beta, alpha, seq, tm=512, tn=512):
    t, d = x_f32.shape
    d_ff = w_down.shape[0]
    tm, tn = _tile(seq, tm), _tile(d_ff, tn)
    n_s, n_j = seq // tm, d_ff // tn
    n_items = (t // tm) * n_j
    hb = tm // HALO
    up_i = lambda s: jnp.minimum(s, n_items - 1) // n_j
    up_j = lambda s: jnp.minimum(s, n_items - 1) % n_j
    dn_i = lambda s: jnp.maximum(s - 1, 0) // n_j
    dn_j = lambda s: jnp.maximum(s - 1, 0) % n_j
    dn_row = lambda s: (dn_i(s), 0)
    return pl.pallas_call(
        functools.partial(_ffn_kernel, tm=tm, n_s=n_s, n_j=n_j, alpha=alpha),
        grid=(n_items + 1,),
        in_specs=[pl.BlockSpec((tm, d), lambda s: (up_i(s), 0)),
                  pl.BlockSpec((HALO, d), lambda s: (jnp.maximum(up_i(s) * hb - 1, 0), 0)),
                  pl.BlockSpec((HALO, d), lambda s: (jnp.minimum((up_i(s) + 1) * hb, t // HALO - 1), 0)),
                  pl.BlockSpec((tm, d), dn_row),
                  pl.BlockSpec((d, tn), lambda s: (0, up_j(s))),
                  pl.BlockSpec((d, tn), lambda s: (0, up_j(s) + n_j)),
                  pl.BlockSpec((3, tn), lambda s: (0, dn_j(s))),
                  pl.BlockSpec((3, tn), lambda s: (0, dn_j(s) + n_j)),
                  pl.BlockSpec((1, tn), lambda s: (0, dn_j(s))),
                  pl.BlockSpec((1, tn), lambda s: (0, dn_j(s) + n_j)),
                  pl.BlockSpec((tn, d), lambda s: (dn_j(s), 0)),
                  _const_spec((1, d)), _const_spec((1, d))],
        out_specs=[pl.BlockSpec((tm, d), dn_row), pl.BlockSpec((tm, d), dn_row)],
        out_shape=[jax.ShapeDtypeStruct((t, d), F32), jax.ShapeDtypeStruct((t, d), BF16)],
        scratch_shapes=[pltpu.VMEM((tm + 2 * HALO, d), BF16),
                        pltpu.VMEM((tm + 2 * HALO, tn), F32), pltpu.VMEM((tm + 2 * HALO, tn), F32)],
        compiler_params=_params("arbitrary"),
        name="conv_ffn_ln",
    )(x_bf16, x_bf16, x_bf16, x_f32, w_up, w_up, conv_w, conv_w,
      conv_b.reshape(1, -1), conv_b.reshape(1, -1), w_down, g.reshape(1, d), beta.reshape(1, d))


def _rms_norm(u, g):
    return u * lax.rsqrt(jnp.mean(u * u, axis=-1, keepdims=True) + RMS_EPS) * g


def _mla_proj_kernel(x_ref, win_ref, gq_ref, gkv_ref, wqn_t_ref, wqr_t_ref, wqrot_t_ref, wuk_ref, wuv_t_ref,
                     tab_ref, qt_ref, k_ref, vt_ref, qmem_ref):
    nt = (((1,), (1,)), ((), ()))
    z = jnp.dot(x_ref[...], win_ref[...], preferred_element_type=F32)
    o1 = Q_LORA_RANK
    o2 = o1 + KV_LORA_RANK
    o3 = o2 + MEM_WIDTH
    o4 = o3 + QK_ROPE_DIM
    qmem_ref[...] = z[:, o2:o3].astype(qmem_ref.dtype)
    qn = _rms_norm(z[:, :o1], gq_ref[...]).astype(BF16)
    ckv = _rms_norm(z[:, o1:o2], gkv_ref[...]).astype(BF16)

    cos_t, sin_t = tab_ref[0:ROPE_FREQS, :], tab_ref[ROPE_FREQS:, :]
    rope_rows = jnp.concatenate([cos_t, cos_t, sin_t, sin_t], axis=0)
    cos2_t, sin2_t = rope_rows[:QK_ROPE_DIM], rope_rows[QK_ROPE_DIM:]
    rope_cols = rope_rows.T

    k_rope = (z[:, o3:o4] * rope_cols[:, :QK_ROPE_DIM] + z[:, o4:o4 + QK_ROPE_DIM] * rope_cols[:, QK_ROPE_DIM:])
    k_rope = k_rope.astype(k_ref.dtype)
    k_nope = jnp.dot(ckv, wuk_ref[...], preferred_element_type=F32)

    qt_nope = lax.dot_general(wqn_t_ref[...], qn, nt, preferred_element_type=F32) * Q_SCALE
    qt_ra = lax.dot_general(wqr_t_ref[...], qn, nt, preferred_element_type=F32)
    qt_rb = lax.dot_general(wqrot_t_ref[...], qn, nt, preferred_element_type=F32)
    vt = lax.dot_general(wuv_t_ref[...], ckv, nt, preferred_element_type=F32)

    for h in range(MLA_HEADS):
        nope = slice(h * QK_NOPE_DIM, (h + 1) * QK_NOPE_DIM)
        rope = slice(h * QK_ROPE_DIM, (h + 1) * QK_ROPE_DIM)
        qt_rope = (qt_ra[rope] * cos2_t + qt_rb[rope] * sin2_t) * Q_SCALE
        qt_ref[0, h, 0:QK_NOPE_DIM, :] = qt_nope[nope].astype(qt_ref.dtype)
        qt_ref[0, h, QK_NOPE_DIM:QK_DIM, :] = qt_rope.astype(qt_ref.dtype)
        k_ref[0, h, :, 0:QK_NOPE_DIM] = k_nope[:, nope].astype(k_ref.dtype)
        k_ref[0, h, :, QK_NOPE_DIM:QK_DIM] = k_rope
        vt_ref[0, h, 0:V_HEAD_DIM, :] = vt[h * V_HEAD_DIM:(h + 1) * V_HEAD_DIM].astype(vt_ref.dtype)
        vt_ref[0, h, V_HEAD_DIM:, :] = jnp.ones((ONES_ROWS, vt.shape[1]), vt_ref.dtype)


def _mla_proj(x_bf16, w_in, gq, gkv, w_qn_t, w_qr_t, w_qrot_t, w_uk, w_uv_t, rope_tab, batch, seq, tm=512):
    t, d = x_bf16.shape
    tm = _tile(seq, tm)
    n_s = seq // tm
    row = lambda width: pl.BlockSpec((tm, width), lambda i: (i, 0))
    rows_blk = lambda width: pl.BlockSpec((1, MLA_HEADS, tm, width), lambda i: (i // n_s, 0, i % n_s, 0))
    cols_blk = lambda height: pl.BlockSpec((1, MLA_HEADS, height, tm), lambda i: (i // n_s, 0, 0, i % n_s))
    consts = [w_in, gq, gkv, w_qn_t, w_qr_t, w_qrot_t, w_uk, w_uv_t]
    return pl.pallas_call(
        _mla_proj_kernel,
        grid=(t // tm,),
        in_specs=([row(d)] + [_const_spec(c.shape) for c in consts]
                  + [pl.BlockSpec((2 * ROPE_FREQS, tm), lambda i: (0, i))]),
        out_specs=[cols_blk(QK_DIM), rows_blk(QK_DIM), cols_blk(V_HEAD_DIM + ONES_ROWS), row(MEM_WIDTH)],
        out_shape=[jax.ShapeDtypeStruct((batch, MLA_HEADS, QK_DIM, seq), BF16),
                   jax.ShapeDtypeStruct((batch, MLA_HEADS, seq, QK_DIM), BF16),
                   jax.ShapeDtypeStruct((batch, MLA_HEADS, V_HEAD_DIM + ONES_ROWS, seq), BF16),
                   jax.ShapeDtypeStruct((t, MEM_WIDTH), BF16)],
        compiler_params=_params("parallel"),
        name="mla_proj",
    )(x_bf16, *consts, rope_tab)


def _flash_kernel(qt_ref, k_ref, vt_ref, o_ref, s0_ref, s1_ref, cm0_ref, cm1_ref, m_ref, acc_ref,
                  *, tk, n_chunks, unroll):
    qt = qt_ref[0, 0]
    s_refs, cm_refs = (s0_ref, s1_ref), (cm0_ref, cm1_ref)
    dv = o_ref.shape[2]

    def scores(j, slot):
        off = pl.multiple_of(j * tk, tk)
        s = jnp.dot(k_ref[0, 0, pl.ds(off, tk), :], qt, preferred_element_type=F32)
        s_refs[slot][...] = s
        cm_refs[slot][...] = jnp.max(s, axis=0, keepdims=True)

    def attend(j, slot):
        off = pl.multiple_of(j * tk, tk)
        m_prev = m_ref[...]
        m_new = jnp.maximum(m_prev, cm_refs[slot][...])
        m_ref[...] = m_new
        p = jnp.exp2(s_refs[slot][...] - m_new).astype(BF16)
        pv = jnp.dot(vt_ref[0, 0, :, pl.ds(off, tk)], p, preferred_element_type=F32)
        acc_ref[...] = jnp.exp2(m_prev - m_new) * acc_ref[...] + pv

    m_ref[...] = jnp.full_like(m_ref, -jnp.inf)
    acc_ref[...] = jnp.zeros_like(acc_ref)
    scores(0, 0)

    def two_chunks(i, carry):
        j = 2 * i
        scores(j + 1, 1)
        attend(j, 0)
        scores(j + 2, 0)
        attend(j + 1, 1)
        return carry

    lax.fori_loop(0, n_chunks // 2 - 1, two_chunks, 0, unroll=unroll)
    scores(n_chunks - 1, 1)
    attend(n_chunks - 2, 0)
    attend(n_chunks - 1, 1)
    o_ref[0] = (acc_ref[0:dv, :] / acc_ref[dv:dv + 1, :]).T.astype(o_ref.dtype)


def _flash_attention(qt, k, vt_ones, tq=512, tk=512, unroll=5):
    batch, heads, dk, seq = qt.shape
    rows = vt_ones.shape[2]
    dv = rows - ONES_ROWS
    tq, tk = _tile(seq, tq), _tile(seq, tk)
    n_chunks = seq // tk
    assert n_chunks >= 2 and n_chunks % 2 == 0, n_chunks
    n_loop = n_chunks // 2 - 1
    return pl.pallas_call(
        functools.partial(_flash_kernel, tk=tk, n_chunks=n_chunks,
                          unroll=unroll if n_loop % unroll == 0 else 1),
        grid=(batch, heads, seq // tq),
        in_specs=[pl.BlockSpec((1, 1, dk, tq), lambda b, h, i: (b, h, 0, i)),
                  pl.BlockSpec((1, 1, seq, dk), lambda b, h, i: (b, h, 0, 0)),
                  pl.BlockSpec((1, 1, rows, seq), lambda b, h, i: (b, h, 0, 0))],
        out_specs=pl.BlockSpec((1, tq, dv), lambda b, h, i: (b, i, h)),
        out_shape=jax.ShapeDtypeStruct((batch, seq, heads * dv), BF16),
        scratch_shapes=[pltpu.VMEM((tk, tq), F32), pltpu.VMEM((tk, tq), F32),
                        pltpu.VMEM((1, tq), F32), pltpu.VMEM((1, tq), F32), pltpu.VMEM((1, tq), F32),
                        pltpu.VMEM((rows, tq), F32)],
        compiler_params=_params("parallel", "parallel", "arbitrary"),
        name="flash_attention",
    )(qt, k, vt_ones)


def _rotate_half_cols(w):
    half = w.shape[-1] // 2
    return jnp.concatenate([-w[..., half:], w[..., :half]], axis=-1)


def _mla_weights(w_in, w_uq):
    o1 = Q_LORA_RANK
    o2 = o1 + KV_LORA_RANK
    o3 = o2 + QK_ROPE_DIM
    k_rope_w = w_in[:, o2:o3]
    w_in_r = jnp.concatenate([w_in[:, :o2], w_in[:, o3:], k_rope_w, _rotate_half_cols(k_rope_w)], axis=1)
    rank = w_uq.shape[0]
    w3 = w_uq.reshape(rank, MLA_HEADS, QK_DIM)
    w_qn = w3[:, :, :QK_NOPE_DIM].reshape(rank, MLA_HEADS * QK_NOPE_DIM)
    w_qr = w3[:, :, QK_NOPE_DIM:]
    w_qrot = _rotate_half_cols(w_qr).reshape(rank, MLA_HEADS * QK_ROPE_DIM)
    w_qr = w_qr.reshape(rank, MLA_HEADS * QK_ROPE_DIM)
    return w_in_r.astype(BF16), w_qn.T.astype(BF16), w_qr.T.astype(BF16), w_qrot.T.astype(BF16)


def kernel(x, mem, positions, pool_w_in, pool_group_w, pool_scale, pool_w_out, mla_w_in, mla_q_norm_g,
           mla_kv_norm_g, mla_w_uq, mla_w_uk, mla_w_uv, mla_w_out, mem_w_kv, ln1_g, ln1_b, ln2_g, ln2_b,
           ffn_w_up, ffn_conv_w, ffn_conv_b, ffn_w_down):
    batch, seq, d = x.shape
    depth = ln1_g.shape[0]
    alpha = (2 * depth) ** 0.25
    t = batch * seq
    token_width = pool_w_in.shape[-1] - MEM_WIDTH

    xf = x.reshape(t, d)
    xb = None
    mem2 = mem.reshape(batch * N_MEM, d)
    rope_tab = _rope_table(positions) if depth > 1 else None

    for i in range(depth):
        j = i // 2
        kv_mem = _matmul(mem2, mem_w_kv[i].astype(BF16), BF16)
        if i % 2 == 0:
            z = _matmul(xf if xb is None else xb, pool_w_in[j].astype(BF16), BF16)
            y_tok = _pool_mix(z, pool_group_w[j].astype(BF16), pool_scale[j], seq)
            y_mem = _mem_attn(z, token_width // MEM_WIDTH, kv_mem, seq)
            w_out = pool_w_out[j].astype(BF16)
        else:
            w_in_r, w_qn_t, w_qr_t, w_qrot_t = _mla_weights(mla_w_in[j], mla_w_uq[j])
            qt, k, vt, q_mem = _mla_proj(
                xb, w_in_r, mla_q_norm_g[j].reshape(1, -1), mla_kv_norm_g[j].reshape(1, -1),
                w_qn_t, w_qr_t, w_qrot_t, mla_w_uk[j].astype(BF16), mla_w_uv[j].T.astype(BF16),
                rope_tab, batch, seq)
            y_tok = _flash_attention(qt, k, vt).reshape(t, MLA_HEADS * V_HEAD_DIM)
            y_mem = _mem_attn(q_mem, 0, kv_mem, seq)
            w_out = mla_w_out[j].astype(BF16)
        split = y_tok.shape[1]
        xf, xb = _outproj_ln(y_tok, y_mem, w_out[:split], w_out[split:], xf, ln1_g[i], ln1_b[i], alpha)
        xf, xb = _conv_ffn_ln(xf, xb, ffn_w_up[i].astype(BF16), ffn_conv_w[i], ffn_conv_b[i],
                              ffn_w_down[i].astype(BF16), ln2_g[i], ln2_b[i], alpha, seq)
    return xf.reshape(batch, seq, d)
```

```python
import functools

import jax
import jax.numpy as jnp
from jax import lax
from jax.experimental import pallas as pl
from jax.experimental.pallas import tpu as pltpu

F32 = jnp.float32
BF16 = jnp.bfloat16

N_MEM = 256
MEM_HEADS = 4
MEM_HEAD_DIM = 128
MEM_WIDTH = MEM_HEADS * MEM_HEAD_DIM
POOL_WINDOWS = (2, 4, 8, 16)
MLA_HEADS = 12
QK_NOPE_DIM = 128
QK_ROPE_DIM = 64
QK_DIM = QK_NOPE_DIM + QK_ROPE_DIM
V_HEAD_DIM = 128
Q_LORA_RANK = 512
KV_LORA_RANK = 512
ROPE_THETA = 10000.0
ROPE_FREQS = QK_ROPE_DIM // 2
MLA_SCALE = QK_DIM ** -0.5
LOG2_E = 1.4426950408889634
Q_SCALE = MLA_SCALE * LOG2_E
MEM_SCALE = MEM_HEAD_DIM ** -0.5
LN_EPS = 1e-5
RMS_EPS = 1e-6

LANES = 128
BF16_SUBLANES = 16
HALO = BF16_SUBLANES
ONES_ROWS = BF16_SUBLANES
VMEM_LIMIT_BYTES = 56 * 1024 * 1024


def _params(*semantics):
    return pltpu.CompilerParams(dimension_semantics=semantics, vmem_limit_bytes=VMEM_LIMIT_BYTES)


def _tile(n, pref):
    t = min(n, pref)
    assert n % t == 0, (n, t)
    return t


def _const_spec(shape):
    nd = len(shape)
    return pl.BlockSpec(shape, lambda *_: (0,) * nd)


def _mm_kernel(x_ref, w_ref, o_ref):
    o_ref[...] = jnp.dot(x_ref[...].astype(BF16), w_ref[...],
                         preferred_element_type=F32).astype(o_ref.dtype)


def _matmul(x, w, out_dtype, tm=1024, tn=1024):
    m, k = x.shape
    n = w.shape[1]
    tm, tn = _tile(m, tm), _tile(n, tn)
    return pl.pallas_call(
        _mm_kernel,
        grid=(m // tm, n // tn),
        in_specs=[pl.BlockSpec((tm, k), lambda i, j: (i, 0)),
                  pl.BlockSpec((k, tn), lambda i, j: (0, j))],
        out_specs=pl.BlockSpec((tm, tn), lambda i, j: (i, j)),
        out_shape=jax.ShapeDtypeStruct((m, n), out_dtype),
        compiler_params=_params("parallel", "arbitrary"),
        name="matmul",
    )(x, w)


def _rope_kernel(pos_ref, invf_ref, tab_ref):
    ang = invf_ref[...] * pos_ref[...].astype(F32)
    tab_ref[0:ROPE_FREQS, :] = jnp.cos(ang)
    tab_ref[ROPE_FREQS:, :] = jnp.sin(ang)


def _rope_table(positions, tm=2048):
    t = positions.size
    tm = _tile(t, tm)
    inv_freq = ROPE_THETA ** (-jnp.arange(0, QK_ROPE_DIM, 2, dtype=F32) / QK_ROPE_DIM)
    return pl.pallas_call(
        _rope_kernel,
        grid=(t // tm,),
        in_specs=[pl.BlockSpec((1, tm), lambda i: (0, i)), _const_spec((ROPE_FREQS, 1))],
        out_specs=pl.BlockSpec((2 * ROPE_FREQS, tm), lambda i: (0, i)),
        out_shape=jax.ShapeDtypeStruct((2 * ROPE_FREQS, t), F32),
        compiler_params=_params("parallel"),
        name="rope_table",
    )(positions.reshape(1, t), inv_freq.reshape(ROPE_FREQS, 1))


def _pool_kernel(zm_ref, zp_ref, zn_ref, gw_ref, sc_ref, o_ref, *, tm, n_s, seq):
    si = pl.program_id(0) % n_s
    first, last = si == 0, si == n_s - 1
    n = tm + 2 * HALO
    gdim = gw_ref.shape[1]
    pos = si * tm + lax.broadcasted_iota(jnp.int32, (tm, 1), 0)
    for k, w in enumerate(POOL_WINDOWS):
        cols = slice(k * gdim, (k + 1) * gdim)
        main = zm_ref[:, cols].astype(F32)
        nxt = jnp.where(last, 0.0, zn_ref[:, cols].astype(F32))
        prv = jnp.where(first, 0.0, zp_ref[:, cols].astype(F32))
        u = jnp.concatenate([main, nxt, prv], axis=0)
        acc = u + pltpu.roll(u, 1, 0)
        half = 1
        while 2 * half < w:
            acc = pltpu.roll(acc, half, 0) + pltpu.roll(acc, n - half, 0)
            half *= 2
        cnt = (jnp.minimum(pos + w // 2, seq) - jnp.maximum(pos - w // 2, 0)).astype(F32)
        pooled = acc[:tm] / cnt - main
        y = jnp.dot(pooled.astype(BF16), gw_ref[k], preferred_element_type=F32)
        o_ref[:, cols] = (y * sc_ref[:, cols]).astype(o_ref.dtype)


def _pool_mix(z, group_w, scale, seq, tm=512):
    t = z.shape[0]
    groups, gdim, _ = group_w.shape
    width = groups * gdim
    tm = _tile(seq, tm)
    n_s = seq // tm
    hb = tm // HALO
    return pl.pallas_call(
        functools.partial(_pool_kernel, tm=tm, n_s=n_s, seq=seq),
        grid=(t // tm,),
        in_specs=[pl.BlockSpec((tm, width), lambda i: (i, 0)),
                  pl.BlockSpec((HALO, width), lambda i: (jnp.maximum(i * hb - 1, 0), 0)),
                  pl.BlockSpec((HALO, width), lambda i: (jnp.minimum((i + 1) * hb, t // HALO - 1), 0)),
                  _const_spec(group_w.shape), _const_spec((1, width))],
        out_specs=pl.BlockSpec((tm, width), lambda i: (i, 0)),
        out_shape=jax.ShapeDtypeStruct((t, width), BF16),
        compiler_params=_params("parallel"),
        name="pool_mix",
    )(z, z, z, group_w, scale.reshape(1, width))


def _mem_attn_kernel(q_ref, kv_ref, o_ref):
    for h in range(MEM_HEADS):
        cols = slice(h * MEM_HEAD_DIM, (h + 1) * MEM_HEAD_DIM)
        vcols = slice(MEM_WIDTH + h * MEM_HEAD_DIM, MEM_WIDTH + (h + 1) * MEM_HEAD_DIM)
        s = lax.dot_general(q_ref[:, cols], kv_ref[:, cols], (((1,), (1,)), ((), ())),
                            preferred_element_type=F32) * MEM_SCALE
        m = jnp.max(s, axis=-1, keepdims=True)
        p = jnp.exp(s - m)
        l = jnp.sum(p, axis=-1, keepdims=True)
        o = jnp.dot(p.astype(BF16), kv_ref[:, vcols], preferred_element_type=F32)
        o_ref[:, cols] = (o / l).astype(o_ref.dtype)


def _mem_attn(q_src, q_col_block, kv, seq, tm=512):
    t = q_src.shape[0]
    tm = _tile(seq, tm)
    n_s = seq // tm
    return pl.pallas_call(
        _mem_attn_kernel,
        grid=(t // tm,),
        in_specs=[pl.BlockSpec((tm, MEM_WIDTH), lambda i: (i, q_col_block)),
                  pl.BlockSpec((N_MEM, 2 * MEM_WIDTH), lambda i: (i // n_s, 0))],
        out_specs=pl.BlockSpec((tm, MEM_WIDTH), lambda i: (i, 0)),
        out_shape=jax.ShapeDtypeStruct((t, MEM_WIDTH), BF16),
        compiler_params=_params("parallel"),
        name="mem_attn",
    )(q_src, kv)


def _layer_norm(u, g, b):
    mu = jnp.mean(u, axis=-1, keepdims=True)
    d = u - mu
    var = jnp.mean(d * d, axis=-1, keepdims=True)
    return d * lax.rsqrt(var + LN_EPS) * g + b


def _outproj_kernel(a_ref, b_ref, wa_ref, wb_ref, x_ref, g_ref, beta_ref, of_ref, ob_ref, *, alpha):
    h = jnp.dot(a_ref[...], wa_ref[...], preferred_element_type=F32)
    h = h + jnp.dot(b_ref[...], wb_ref[...], preferred_element_type=F32)
    y = _layer_norm(alpha * x_ref[...] + h, g_ref[...], beta_ref[...])
    of_ref[...] = y
    ob_ref[...] = y.astype(BF16)


def _outproj_ln(a, b, w_a, w_b, x, g, beta, alpha, tm=512):
    t, d = x.shape
    tm = _tile(t, tm)
    ka, kb = a.shape[1], b.shape[1]
    row = lambda width: pl.BlockSpec((tm, width), lambda i: (i, 0))
    return pl.pallas_call(
        functools.partial(_outproj_kernel, alpha=alpha),
        grid=(t // tm,),
        in_specs=[row(ka), row(kb), _const_spec((ka, d)), _const_spec((kb, d)), row(d),
                  _const_spec((1, d)), _const_spec((1, d))],
        out_specs=[row(d), row(d)],
        out_shape=[jax.ShapeDtypeStruct((t, d), F32), jax.ShapeDtypeStruct((t, d), BF16)],
        compiler_params=_params("parallel"),
        name="outproj_ln",
    )(a, b, w_a, w_b, x, g.reshape(1, d), beta.reshape(1, d))


def _ffn_kernel(xm_ref, xp_ref, xn_ref, xres_ref, wg_ref, wv_ref, cwg_ref, cwv_ref, cbg_ref, cbv_ref,
                wd_ref, g_ref, beta_ref, of_ref, ob_ref, xext_ref, hg_ref, hv_ref, *, tm, n_s, n_j, alpha):
    s = pl.program_id(0)
    last = pl.num_programs(0) - 1
    up_item = jnp.minimum(s, last - 1)
    dn_item = jnp.maximum(s - 1, 0)
    jb = dn_item % n_j
    n = tm + HALO
    edge = HALO // 2

    @pl.when(s == 0)
    def _():
        hg_ref[...] = jnp.zeros_like(hg_ref)
        hv_ref[...] = jnp.zeros_like(hv_ref)

    @pl.when(up_item % n_j == 0)
    def _():
        si = (up_item // n_j) % n_s
        xext_ref[0:tm, :] = xm_ref[...]
        nxt = jnp.where(si == n_s - 1, 0.0, xn_ref[...].astype(F32)[:edge])
        prv = jnp.where(si == 0, 0.0, xp_ref[...].astype(F32)[HALO - edge:])
        xext_ref[tm:n, :] = jnp.concatenate([nxt, prv], axis=0).astype(xext_ref.dtype)

    @pl.when(jb == 0)
    def _():
        of_ref[...] = jnp.zeros_like(of_ref)

    def conv_branch(h, cw_ref, cb_ref):
        c = (pltpu.roll(h, 1, 0) * cw_ref[0:1, :] + h * cw_ref[1:2, :]
             + pltpu.roll(h, n - 1, 0) * cw_ref[2:3, :])
        return c[:tm] + cb_ref[...]

    h_gate, h_val = hg_ref[...], hv_ref[...]
    xe = xext_ref[...]
    hg_ref[...] = jnp.dot(xe, wg_ref[...], preferred_element_type=F32)
    hv_ref[...] = jnp.dot(xe, wv_ref[...], preferred_element_type=F32)

    gate = conv_branch(h_gate, cwg_ref, cbg_ref)
    val = conv_branch(h_val, cwv_ref, cbv_ref)
    act = (gate * jax.nn.sigmoid(gate) * val).astype(BF16)
    of_ref[...] += jnp.dot(act, wd_ref[...], preferred_element_type=F32)

    @pl.when((jb == n_j - 1) & (s > 0))
    def _():
        y = _layer_norm(alpha * xres_ref[...] + of_ref[...], g_ref[...], beta_ref[...])
        of_ref[...] = y
        ob_ref[...] = y.astype(BF16)


def _conv_ffn_ln(x_f32, x_bf16, w_up, conv_w, conv_b, w_down, g, beta, alpha, seq, tm=512, tn=512):
    t, d = x_f32.shape
    d_ff = w_down.shape[0]
    tm, tn = _tile(seq, tm), _tile(d_ff, tn)
    n_s, n_j = seq // tm, d_ff // tn
    n_items = (t // tm) * n_j
    hb = tm // HALO
    up_i = lambda s: jnp.minimum(s, n_items - 1) // n_j
    up_j = lambda s: jnp.minimum(s, n_items - 1) % n_j
    dn_i = lambda s: jnp.maximum(s - 1, 0) // n_j
    dn_j = lambda s: jnp.maximum(s - 1, 0) % n_j
    dn_row = lambda s: (dn_i(s), 0)
    return pl.pallas_call(
        functools.partial(_ffn_kernel, tm=tm, n_s=n_s, n_j=n_j, alpha=alpha),
        grid=(n_items + 1,),
        in_specs=[pl.BlockSpec((tm, d), lambda s: (up_i(s), 0)),
                  pl.BlockSpec((HALO, d), lambda s: (jnp.maximum(up_i(s) * hb - 1, 0), 0)),
                  pl.BlockSpec((HALO, d), lambda s: (jnp.minimum((up_i(s) + 1) * hb, t // HALO - 1), 0)),
                  pl.BlockSpec((tm, d), dn_row),
                  pl.BlockSpec((d, tn), lambda s: (0, up_j(s))),
                  pl.BlockSpec((d, tn), lambda s: (0, up_j(s) + n_j)),
                  pl.BlockSpec((3, tn), lambda s: (0, dn_j(s))),
                  pl.BlockSpec((3, tn), lambda s: (0, dn_j(s) + n_j)),
                  pl.BlockSpec((1, tn), lambda s: (0, dn_j(s))),
                  pl.BlockSpec((1, tn), lambda s: (0, dn_j(s) + n_j)),
                  pl.BlockSpec((tn, d), lambda s: (dn_j(s), 0)),
                  _const_spec((1, d)), _const_spec((1, d))],
        out_specs=[pl.BlockSpec((tm, d), dn_row), pl.BlockSpec((tm, d), dn_row)],
        out_shape=[jax.ShapeDtypeStruct((t, d), F32), jax.ShapeDtypeStruct((t, d), BF16)],
        scratch_shapes=[pltpu.VMEM((tm + HALO, d), BF16),
                        pltpu.VMEM((tm + HALO, tn), F32), pltpu.VMEM((tm + HALO, tn), F32)],
        compiler_params=_params("arbitrary"),
        name="conv_ffn_ln",
    )(x_bf16, x_bf16, x_bf16, x_f32, w_up, w_up, conv_w, conv_w,
      conv_b.reshape(1, -1), conv_b.reshape(1, -1), w_down, g.reshape(1, d), beta.reshape(1, d))


def _rms_norm(u, g):
    return u * lax.rsqrt(jnp.mean(u * u, axis=-1, keepdims=True) + RMS_EPS) * g


def _mla_proj_kernel(x_ref, win_ref, gq_ref, gkv_ref, wqn_t_ref, wqr_t_ref, wqrot_t_ref, wuk_ref, wuv_t_ref,
                     tab_ref, qt_ref, k_ref, vt_ref, qmem_ref):
    nt = (((1,), (1,)), ((), ()))
    z = jnp.dot(x_ref[...], win_ref[...], preferred_element_type=F32)
    o1 = Q_LORA_RANK
    o2 = o1 + KV_LORA_RANK
    o3 = o2 + MEM_WIDTH
    o4 = o3 + QK_ROPE_DIM
    qmem_ref[...] = z[:, o2:o3].astype(qmem_ref.dtype)
    qn = _rms_norm(z[:, :o1], gq_ref[...]).astype(BF16)
    ckv = _rms_norm(z[:, o1:o2], gkv_ref[...]).astype(BF16)

    cos_t, sin_t = tab_ref[0:ROPE_FREQS, :], tab_ref[ROPE_FREQS:, :]
    rope_rows = jnp.concatenate([cos_t, cos_t, sin_t, sin_t], axis=0)
    cos2_t, sin2_t = rope_rows[:QK_ROPE_DIM], rope_rows[QK_ROPE_DIM:]
    rope_cols = rope_rows.T

    k_rope = (z[:, o3:o4] * rope_cols[:, :QK_ROPE_DIM] + z[:, o4:o4 + QK_ROPE_DIM] * rope_cols[:, QK_ROPE_DIM:])
    k_rope = k_rope.astype(k_ref.dtype)
    k_nope = jnp.dot(ckv, wuk_ref[...], preferred_element_type=F32)

    qt_nope = lax.dot_general(wqn_t_ref[...], qn, nt, preferred_element_type=F32) * Q_SCALE
    qt_ra = lax.dot_general(wqr_t_ref[...], qn, nt, preferred_element_type=F32)
    qt_rb = lax.dot_general(wqrot_t_ref[...], qn, nt, preferred_element_type=F32)
    vt = lax.dot_general(wuv_t_ref[...], ckv, nt, preferred_element_type=F32)

    for h in range(MLA_HEADS):
        nope = slice(h * QK_NOPE_DIM, (h + 1) * QK_NOPE_DIM)
        rope = slice(h * QK_ROPE_DIM, (h + 1) * QK_ROPE_DIM)
        qt_rope = (qt_ra[rope] * cos2_t + qt_rb[rope] * sin2_t) * Q_SCALE
        qt_ref[0, h, 0:QK_NOPE_DIM, :] = qt_nope[nope].astype(qt_ref.dtype)
        qt_ref[0, h, QK_NOPE_DIM:QK_DIM, :] = qt_rope.astype(qt_ref.dtype)
        k_ref[0, h, :, 0:QK_NOPE_DIM] = k_nope[:, nope].astype(k_ref.dtype)
        k_ref[0, h, :, QK_NOPE_DIM:QK_DIM] = k_rope
        vt_ref[0, h, 0:V_HEAD_DIM, :] = vt[h * V_HEAD_DIM:(h + 1) * V_HEAD_DIM].astype(vt_ref.dtype)
        vt_ref[0, h, V_HEAD_DIM:, :] = jnp.ones((ONES_ROWS, vt.shape[1]), vt_ref.dtype)


def _mla_proj(x_bf16, w_in, gq, gkv, w_qn_t, w_qr_t, w_qrot_t, w_uk, w_uv_t, rope_tab, batch, seq, tm=512):
    t, d = x_bf16.shape
    tm = _tile(seq, tm)
    n_s = seq // tm
    row = lambda width: pl.BlockSpec((tm, width), lambda i: (i, 0))
    rows_blk = lambda width: pl.BlockSpec((1, MLA_HEADS, tm, width), lambda i: (i // n_s, 0, i % n_s, 0))
    cols_blk = lambda height: pl.BlockSpec((1, MLA_HEADS, height, tm), lambda i: (i // n_s, 0, 0, i % n_s))
    consts = [w_in, gq, gkv, w_qn_t, w_qr_t, w_qrot_t, w_uk, w_uv_t]
    return pl.pallas_call(
        _mla_proj_kernel,
        grid=(t // tm,),
        in_specs=([row(d)] + [_const_spec(c.shape) for c in consts]
                  + [pl.BlockSpec((2 * ROPE_FREQS, tm), lambda i: (0, i))]),
        out_specs=[cols_blk(QK_DIM), rows_blk(QK_DIM), cols_blk(V_HEAD_DIM + ONES_ROWS), row(MEM_WIDTH)],
        out_shape=[jax.ShapeDtypeStruct((batch, MLA_HEADS, QK_DIM, seq), BF16),
                   jax.ShapeDtypeStruct((batch, MLA_HEADS, seq, QK_DIM), BF16),
                   jax.ShapeDtypeStruct((batch, MLA_HEADS, V_HEAD_DIM + ONES_ROWS, seq), BF16),
                   jax.ShapeDtypeStruct((t, MEM_WIDTH), BF16)],
        compiler_params=_params("parallel"),
        name="mla_proj",
    )(x_bf16, *consts, rope_tab)


def _flash_kernel(qt_ref, qt_next_ref, k_ref, vt_ref, o_ref, s0_ref, s1_ref, cm0_ref, cm1_ref, m_ref, acc_ref,
                  *, tk, n_chunks, unroll):
    s_refs, cm_refs = (s0_ref, s1_ref), (cm0_ref, cm1_ref)
    dv = o_ref.shape[2]

    def scores(j, slot, q_ref=qt_ref):
        off = pl.multiple_of(j * tk, tk)
        s = jnp.dot(k_ref[0, 0, pl.ds(off, tk), :], q_ref[0, 0], preferred_element_type=F32)
        s_refs[slot][...] = s
        cm_refs[slot][...] = jnp.max(s, axis=0, keepdims=True)

    def attend(j, slot):
        off = pl.multiple_of(j * tk, tk)
        m_prev = m_ref[...]
        m_new = jnp.maximum(m_prev, cm_refs[slot][...])
        m_ref[...] = m_new
        p = jnp.exp2(s_refs[slot][...] - m_new).astype(BF16)
        pv = jnp.dot(vt_ref[0, 0, :, pl.ds(off, tk)], p, preferred_element_type=F32)
        acc_ref[...] = jnp.exp2(m_prev - m_new) * acc_ref[...] + pv

    @pl.when(pl.program_id(2) == 0)
    def _():
        scores(0, 0)

    m_ref[...] = jnp.full_like(m_ref, -jnp.inf)
    acc_ref[...] = jnp.zeros_like(acc_ref)

    def two_chunks(i, carry):
        j = 2 * i
        scores(j + 1, 1)
        attend(j, 0)
        scores(j + 2, 0)
        attend(j + 1, 1)
        return carry

    lax.fori_loop(0, n_chunks // 2 - 1, two_chunks, 0, unroll=unroll)
    scores(n_chunks - 1, 1)
    attend(n_chunks - 2, 0)
    scores(0, 0, qt_next_ref)
    attend(n_chunks - 1, 1)
    o_ref[0] = (acc_ref[0:dv, :] / acc_ref[dv:dv + 1, :]).T.astype(o_ref.dtype)


def _flash_attention(qt, k, vt_ones, tq=512, tk=512, unroll=5):
    batch, heads, dk, seq = qt.shape
    rows = vt_ones.shape[2]
    dv = rows - ONES_ROWS
    tq, tk = _tile(seq, tq), _tile(seq, tk)
    n_chunks, n_q = seq // tk, seq // tq
    assert n_chunks >= 2 and n_chunks % 2 == 0, n_chunks
    n_loop = n_chunks // 2 - 1
    return pl.pallas_call(
        functools.partial(_flash_kernel, tk=tk, n_chunks=n_chunks,
                          unroll=unroll if n_loop % unroll == 0 else 1),
        grid=(batch, heads, n_q),
        in_specs=[pl.BlockSpec((1, 1, dk, tq), lambda b, h, i: (b, h, 0, i)),
                  pl.BlockSpec((1, 1, dk, tq), lambda b, h, i: (b, h, 0, jnp.minimum(i + 1, n_q - 1))),
                  pl.BlockSpec((1, 1, seq, dk), lambda b, h, i: (b, h, 0, 0)),
                  pl.BlockSpec((1, 1, rows, seq), lambda b, h, i: (b, h, 0, 0))],
        out_specs=pl.BlockSpec((1, tq, dv), lambda b, h, i: (b, i, h)),
        out_shape=jax.ShapeDtypeStruct((batch, seq, heads * dv), BF16),
        scratch_shapes=[pltpu.VMEM((tk, tq), F32), pltpu.VMEM((tk, tq), F32),
                        pltpu.VMEM((1, tq), F32), pltpu.VMEM((1, tq), F32), pltpu.VMEM((1, tq), F32),
                        pltpu.VMEM((rows, tq), F32)],
        compiler_params=_params("parallel", "parallel", "arbitrary"),
        name="flash_attention",
    )(qt, qt, k, vt_ones)


def _rotate_half_cols(w):
    half = w.shape[-1] // 2
    return jnp.concatenate([-w[..., half:], w[..., :half]], axis=-1)


def _mla_weights(w_in, w_uq):
    o1 = Q_LORA_RANK
    o2 = o1 + KV_LORA_RANK
    o3 = o2 + QK_ROPE_DIM
    k_rope_w = w_in[:, o2:o3]
    w_in_r = jnp.concatenate([w_in[:, :o2], w_in[:, o3:], k_rope_w, _rotate_half_cols(k_rope_w)], axis=1)
    rank = w_uq.shape[0]
    w3 = w_uq.reshape(rank, MLA_HEADS, QK_DIM)
    w_qn = w3[:, :, :QK_NOPE_DIM].reshape(rank, MLA_HEADS * QK_NOPE_DIM)
    w_qr = w3[:, :, QK_NOPE_DIM:]
    w_qrot = _rotate_half_cols(w_qr).reshape(rank, MLA_HEADS * QK_ROPE_DIM)
    w_qr = w_qr.reshape(rank, MLA_HEADS * QK_ROPE_DIM)
    return w_in_r.astype(BF16), w_qn.T.astype(BF16), w_qr.T.astype(BF16), w_qrot.T.astype(BF16)


def kernel(x, mem, positions, pool_w_in, pool_group_w, pool_scale, pool_w_out, mla_w_in, mla_q_norm_g,
           mla_kv_norm_g, mla_w_uq, mla_w_uk, mla_w_uv, mla_w_out, mem_w_kv, ln1_g, ln1_b, ln2_g, ln2_b,
           ffn_w_up, ffn_conv_w, ffn_conv_b, ffn_w_down):
    batch, seq, d = x.shape
    depth = ln1_g.shape[0]
    alpha = (2 * depth) ** 0.25
    t = batch * seq
    token_width = pool_w_in.shape[-1] - MEM_WIDTH

    xf = x.reshape(t, d)
    xb = None
    mem2 = mem.reshape(batch * N_MEM, d)
    rope_tab = _rope_table(positions) if depth > 1 else None

    for i in range(depth):
        j = i // 2
        kv_mem = _matmul(mem2, mem_w_kv[i].astype(BF16), BF16)
        if i % 2 == 0:
            z = _matmul(xf if xb is None else xb, pool_w_in[j].astype(BF16), BF16)
            y_tok = _pool_mix(z, pool_group_w[j].astype(BF16), pool_scale[j], seq)
            y_mem = _mem_attn(z, token_width // MEM_WIDTH, kv_mem, seq)
            w_out = pool_w_out[j].astype(BF16)
        else:
            w_in_r, w_qn_t, w_qr_t, w_qrot_t = _mla_weights(mla_w_in[j], mla_w_uq[j])
            qt, k, vt, q_mem = _mla_proj(
                xb, w_in_r, mla_q_norm_g[j].reshape(1, -1), mla_kv_norm_g[j].reshape(1, -1),
                w_qn_t, w_qr_t, w_qrot_t, mla_w_uk[j].astype(BF16), mla_w_uv[j].T.astype(BF16),
                rope_tab, batch, seq)
            y_tok = _flash_attention(qt, k, vt).reshape(t, MLA_HEADS * V_HEAD_DIM)
            y_mem = _mem_attn(q_mem, 0, kv_mem, seq)
            w_out = mla_w_out[j].astype(BF16)
        split = y_tok.shape[1]
        xf, xb = _outproj_ln(y_tok, y_mem, w_out[:split], w_out[split:], xf, ln1_g[i], ln1_b[i], alpha)
        xf, xb = _conv_ffn_ln(xf, xb, ffn_w_up[i].astype(BF16), ffn_conv_w[i], ffn_conv_b[i],
                              ffn_w_down[i].astype(BF16), ln2_g[i], ln2_b[i], alpha, seq)
    return xf.reshape(batch, seq, d)
```

```python
import functools

import jax
import jax.numpy as jnp
from jax import lax
from jax.experimental import pallas as pl
from jax.experimental.pallas import tpu as pltpu

F32 = jnp.float32
BF16 = jnp.bfloat16

N_MEM = 256
MEM_HEADS = 4
MEM_HEAD_DIM = 128
MEM_WIDTH = MEM_HEADS * MEM_HEAD_DIM
POOL_WINDOWS = (2, 4, 8, 16)
MLA_HEADS = 12
QK_NOPE_DIM = 128
QK_ROPE_DIM = 64
QK_DIM = QK_NOPE_DIM + QK_ROPE_DIM
V_HEAD_DIM = 128
Q_LORA_RANK = 512
KV_LORA_RANK = 512
ROPE_THETA = 10000.0
ROPE_FREQS = QK_ROPE_DIM // 2
MLA_SCALE = QK_DIM ** -0.5
LOG2_E = 1.4426950408889634
Q_SCALE = MLA_SCALE * LOG2_E
MEM_SCALE = MEM_HEAD_DIM ** -0.5
LN_EPS = 1e-5
RMS_EPS = 1e-6

LANES = 128
BF16_SUBLANES = 16
HALO = BF16_SUBLANES
ONES_ROWS = BF16_SUBLANES
VMEM_LIMIT_BYTES = 56 * 1024 * 1024


def _params(*semantics):
    return pltpu.CompilerParams(dimension_semantics=semantics, vmem_limit_bytes=VMEM_LIMIT_BYTES)


def _tile(n, pref):
    t = min(n, pref)
    assert n % t == 0, (n, t)
    return t


def _const_spec(shape):
    nd = len(shape)
    return pl.BlockSpec(shape, lambda *_: (0,) * nd)


def _mm_kernel(x_ref, w_ref, o_ref):
    o_ref[...] = jnp.dot(x_ref[...].astype(BF16), w_ref[...],
                         preferred_element_type=F32).astype(o_ref.dtype)


def _matmul(x, w, out_dtype, tm=1024, tn=1024):
    m, k = x.shape
    n = w.shape[1]
    tm, tn = _tile(m, tm), _tile(n, tn)
    return pl.pallas_call(
        _mm_kernel,
        grid=(m // tm, n // tn),
        in_specs=[pl.BlockSpec((tm, k), lambda i, j: (i, 0)),
                  pl.BlockSpec((k, tn), lambda i, j: (0, j))],
        out_specs=pl.BlockSpec((tm, tn), lambda i, j: (i, j)),
        out_shape=jax.ShapeDtypeStruct((m, n), out_dtype),
        compiler_params=_params("parallel", "arbitrary"),
        name="matmul",
    )(x, w)


def _rope_kernel(pos_ref, invf_ref, tab_ref):
    ang = invf_ref[...] * pos_ref[...].astype(F32)
    tab_ref[0:ROPE_FREQS, :] = jnp.cos(ang)
    tab_ref[ROPE_FREQS:, :] = jnp.sin(ang)


def _rope_table(positions, tm=2048):
    t = positions.size
    tm = _tile(t, tm)
    inv_freq = ROPE_THETA ** (-jnp.arange(0, QK_ROPE_DIM, 2, dtype=F32) / QK_ROPE_DIM)
    return pl.pallas_call(
        _rope_kernel,
        grid=(t // tm,),
        in_specs=[pl.BlockSpec((1, tm), lambda i: (0, i)), _const_spec((ROPE_FREQS, 1))],
        out_specs=pl.BlockSpec((2 * ROPE_FREQS, tm), lambda i: (0, i)),
        out_shape=jax.ShapeDtypeStruct((2 * ROPE_FREQS, t), F32),
        compiler_params=_params("parallel"),
        name="rope_table",
    )(positions.reshape(1, t), inv_freq.reshape(ROPE_FREQS, 1))


def _pool_kernel(zm_ref, zp_ref, zn_ref, gw_ref, sc_ref, o_ref, *, tm, n_s, seq):
    si = pl.program_id(0) % n_s
    first, last = si == 0, si == n_s - 1
    n = tm + 2 * HALO
    gdim = gw_ref.shape[1]
    pos = si * tm + lax.broadcasted_iota(jnp.int32, (tm, 1), 0)
    for k, w in enumerate(POOL_WINDOWS):
        cols = slice(k * gdim, (k + 1) * gdim)
        main = zm_ref[:, cols].astype(F32)
        nxt = jnp.where(last, 0.0, zn_ref[:, cols].astype(F32))
        prv = jnp.where(first, 0.0, zp_ref[:, cols].astype(F32))
        u = jnp.concatenate([main, nxt, prv], axis=0)
        acc = u + pltpu.roll(u, 1, 0)
        half = 1
        while 2 * half < w:
            acc = pltpu.roll(acc, half, 0) + pltpu.roll(acc, n - half, 0)
            half *= 2
        cnt = (jnp.minimum(pos + w // 2, seq) - jnp.maximum(pos - w // 2, 0)).astype(F32)
        pooled = acc[:tm] / cnt - main
        y = jnp.dot(pooled.astype(BF16), gw_ref[k], preferred_element_type=F32)
        o_ref[:, cols] = (y * sc_ref[:, cols]).astype(o_ref.dtype)


def _pool_mix(z, group_w, scale, seq, tm=512):
    t = z.shape[0]
    groups, gdim, _ = group_w.shape
    width = groups * gdim
    tm = _tile(seq, tm)
    n_s = seq // tm
    hb = tm // HALO
    return pl.pallas_call(
        functools.partial(_pool_kernel, tm=tm, n_s=n_s, seq=seq),
        grid=(t // tm,),
        in_specs=[pl.BlockSpec((tm, width), lambda i: (i, 0)),
                  pl.BlockSpec((HALO, width), lambda i: (jnp.maximum(i * hb - 1, 0), 0)),
                  pl.BlockSpec((HALO, width), lambda i: (jnp.minimum((i + 1) * hb, t // HALO - 1), 0)),
                  _const_spec(group_w.shape), _const_spec((1, width))],
        out_specs=pl.BlockSpec((tm, width), lambda i: (i, 0)),
        out_shape=jax.ShapeDtypeStruct((t, width), BF16),
        compiler_params=_params("parallel"),
        name="pool_mix",
    )(z, z, z, group_w, scale.reshape(1, width))


def _mem_attn_kernel(q_ref, kv_ref, o_ref):
    for h in range(MEM_HEADS):
        cols = slice(h * MEM_HEAD_DIM, (h + 1) * MEM_HEAD_DIM)
        vcols = slice(MEM_WIDTH + h * MEM_HEAD_DIM, MEM_WIDTH + (h + 1) * MEM_HEAD_DIM)
        s = lax.dot_general(q_ref[:, cols], kv_ref[:, cols], (((1,), (1,)), ((), ())),
                            preferred_element_type=F32) * MEM_SCALE
        m = jnp.max(s, axis=-1, keepdims=True)
        p = jnp.exp(s - m)
        l = jnp.sum(p, axis=-1, keepdims=True)
        o = jnp.dot(p.astype(BF16), kv_ref[:, vcols], preferred_element_type=F32)
        o_ref[:, cols] = (o / l).astype(o_ref.dtype)


def _mem_attn(q_src, q_col_block, kv, seq, tm=512):
    t = q_src.shape[0]
    tm = _tile(seq, tm)
    n_s = seq // tm
    return pl.pallas_call(
        _mem_attn_kernel,
        grid=(t // tm,),
        in_specs=[pl.BlockSpec((tm, MEM_WIDTH), lambda i: (i, q_col_block)),
                  pl.BlockSpec((N_MEM, 2 * MEM_WIDTH), lambda i: (i // n_s, 0))],
        out_specs=pl.BlockSpec((tm, MEM_WIDTH), lambda i: (i, 0)),
        out_shape=jax.ShapeDtypeStruct((t, MEM_WIDTH), BF16),
        compiler_params=_params("parallel"),
        name="mem_attn",
    )(q_src, kv)


def _layer_norm(u, g, b):
    mu = jnp.mean(u, axis=-1, keepdims=True)
    d = u - mu
    var = jnp.mean(d * d, axis=-1, keepdims=True)
    return d * lax.rsqrt(var + LN_EPS) * g + b


def _outproj_kernel(a_ref, b_ref, wa_ref, wb_ref, x_ref, g_ref, beta_ref, of_ref, ob_ref, *, alpha):
    h = jnp.dot(a_ref[...], wa_ref[...], preferred_element_type=F32)
    h = h + jnp.dot(b_ref[...], wb_ref[...], preferred_element_type=F32)
    y = _layer_norm(alpha * x_ref[...] + h, g_ref[...], beta_ref[...])
    of_ref[...] = y
    ob_ref[...] = y.astype(BF16)


def _outproj_ln(a, b, w_a, w_b, x, g, beta, alpha, tm=512):
    t, d = x.shape
    tm = _tile(t, tm)
    ka, kb = a.shape[1], b.shape[1]
    row = lambda width: pl.BlockSpec((tm, width), lambda i: (i, 0))
    return pl.pallas_call(
        functools.partial(_outproj_kernel, alpha=alpha),
        grid=(t // tm,),
        in_specs=[row(ka), row(kb), _const_spec((ka, d)), _const_spec((kb, d)), row(d),
                  _const_spec((1, d)), _const_spec((1, d))],
        out_specs=[row(d), row(d)],
        out_shape=[jax.ShapeDtypeStruct((t, d), F32), jax.ShapeDtypeStruct((t, d), BF16)],
        compiler_params=_params("parallel"),
        name="outproj_ln",
    )(a, b, w_a, w_b, x, g.reshape(1, d), beta.reshape(1, d))


def _ffn_kernel(xm_ref, xp_ref, xn_ref, xres_ref, wg_ref, wv_ref, cwg_ref, cwv_ref, cbg_ref, cbv_ref,
                wd_ref, g_ref, beta_ref, of_ref, ob_ref, xext_ref, hg_ref, hv_ref, *, tm, n_s, n_j, alpha):
    s = pl.program_id(0)
    last = pl.num_programs(0) - 1
    up_item = jnp.minimum(s, last - 1)
    dn_item = jnp.maximum(s - 1, 0)
    jb = dn_item % n_j
    n = tm + HALO
    edge = HALO // 2

    @pl.when(s == 0)
    def _():
        hg_ref[...] = jnp.zeros_like(hg_ref)
        hv_ref[...] = jnp.zeros_like(hv_ref)

    @pl.when(up_item % n_j == 0)
    def _():
        si = (up_item // n_j) % n_s
        xext_ref[0:tm, :] = xm_ref[...]
        nxt = jnp.where(si == n_s - 1, 0.0, xn_ref[...].astype(F32)[:edge])
        prv = jnp.where(si == 0, 0.0, xp_ref[...].astype(F32)[HALO - edge:])
        xext_ref[tm:n, :] = jnp.concatenate([nxt, prv], axis=0).astype(xext_ref.dtype)

    @pl.when(jb == 0)
    def _():
        of_ref[...] = jnp.zeros_like(of_ref)

    def conv_branch(h, cw_ref, cb_ref):
        c = (pltpu.roll(h, 1, 0) * cw_ref[0:1, :] + h * cw_ref[1:2, :]
             + pltpu.roll(h, n - 1, 0) * cw_ref[2:3, :])
        return c[:tm] + cb_ref[...]

    h_gate, h_val = hg_ref[...], hv_ref[...]
    xe = xext_ref[...]
    hg_ref[...] = jnp.dot(xe, wg_ref[...], preferred_element_type=F32)
    hv_ref[...] = jnp.dot(xe, wv_ref[...], preferred_element_type=F32)

    gate = conv_branch(h_gate, cwg_ref, cbg_ref)
    val = conv_branch(h_val, cwv_ref, cbv_ref)
    act = (gate * jax.nn.sigmoid(gate) * val).astype(BF16)
    of_ref[...] += jnp.dot(act, wd_ref[...], preferred_element_type=F32)

    @pl.when((jb == n_j - 1) & (s > 0))
    def _():
        y = _layer_norm(alpha * xres_ref[...] + of_ref[...], g_ref[...], beta_ref[...])
        of_ref[...] = y
        ob_ref[...] = y.astype(BF16)


def _conv_ffn_ln(x_f32, x_bf16, w_up, conv_w, conv_b, w_down, g, beta, alpha, seq, tm=512, tn=512):
    t, d = x_f32.shape
    d_ff = w_down.shape[0]
    tm, tn = _tile(seq, tm), _tile(d_ff, tn)
    n_s, n_j = seq // tm, d_ff // tn
    n_items = (t // tm) * n_j
    hb = tm // HALO
    up_i = lambda s: jnp.minimum(s, n_items - 1) // n_j
    up_j = lambda s: jnp.minimum(s, n_items - 1) % n_j
    dn_i = lambda s: jnp.maximum(s - 1, 0) // n_j
    dn_j = lambda s: jnp.maximum(s - 1, 0) % n_j
    dn_row = lambda s: (dn_i(s), 0)
    return pl.pallas_call(
        functools.partial(_ffn_kernel, tm=tm, n_s=n_s, n_j=n_j, alpha=alpha),
        grid=(n_items + 1,),
        in_specs=[pl.BlockSpec((tm, d), lambda s: (up_i(s), 0)),
                  pl.BlockSpec((HALO, d), lambda s: (jnp.maximum(up_i(s) * hb - 1, 0), 0)),
                  pl.BlockSpec((HALO, d), lambda s: (jnp.minimum((up_i(s) + 1) * hb, t // HALO - 1), 0)),
                  pl.BlockSpec((tm, d), dn_row),
                  pl.BlockSpec((d, tn), lambda s: (0, up_j(s))),
                  pl.BlockSpec((d, tn), lambda s: (0, up_j(s) + n_j)),
                  pl.BlockSpec((3, tn), lambda s: (0, dn_j(s))),
                  pl.BlockSpec((3, tn), lambda s: (0, dn_j(s) + n_j)),
                  pl.BlockSpec((1, tn), lambda s: (0, dn_j(s))),
                  pl.BlockSpec((1, tn), lambda s: (0, dn_j(s) + n_j)),
                  pl.BlockSpec((tn, d), lambda s: (dn_j(s), 0)),
                  _const_spec((1, d)), _const_spec((1, d))],
        out_specs=[pl.BlockSpec((tm, d), dn_row), pl.BlockSpec((tm, d), dn_row)],
        out_shape=[jax.ShapeDtypeStruct((t, d), F32), jax.ShapeDtypeStruct((t, d), BF16)],
        scratch_shapes=[pltpu.VMEM((tm + HALO, d), BF16),
                        pltpu.VMEM((tm + HALO, tn), F32), pltpu.VMEM((tm + HALO, tn), F32)],
        compiler_params=_params("arbitrary"),
        name="conv_ffn_ln",
    )(x_bf16, x_bf16, x_bf16, x_f32, w_up, w_up, conv_w, conv_w,
      conv_b.reshape(1, -1), conv_b.reshape(1, -1), w_down, g.reshape(1, d), beta.reshape(1, d))


def _rms_norm(u, g):
    return u * lax.rsqrt(jnp.mean(u * u, axis=-1, keepdims=True) + RMS_EPS) * g


def _mla_proj_kernel(x_ref, win_ref, gq_ref, gkv_ref, wqn_t_ref, wqr_t_ref, wqrot_t_ref, wuk_ref, wuv_t_ref,
                     tab_ref, qt_ref, k_ref, vt_ref, qmem_ref):
    nt = (((1,), (1,)), ((), ()))
    z = jnp.dot(x_ref[...], win_ref[...], preferred_element_type=F32)
    o1 = Q_LORA_RANK
    o2 = o1 + KV_LORA_RANK
    o3 = o2 + MEM_WIDTH
    o4 = o3 + QK_ROPE_DIM
    qmem_ref[...] = z[:, o2:o3].astype(qmem_ref.dtype)
    qn = _rms_norm(z[:, :o1], gq_ref[...]).astype(BF16)
    ckv = _rms_norm(z[:, o1:o2], gkv_ref[...]).astype(BF16)

    cos_t, sin_t = tab_ref[0:ROPE_FREQS, :], tab_ref[ROPE_FREQS:, :]
    rope_rows = jnp.concatenate([cos_t, cos_t, sin_t, sin_t], axis=0)
    cos2_t, sin2_t = rope_rows[:QK_ROPE_DIM], rope_rows[QK_ROPE_DIM:]
    rope_cols = rope_rows.T

    k_rope = (z[:, o3:o4] * rope_cols[:, :QK_ROPE_DIM] + z[:, o4:o4 + QK_ROPE_DIM] * rope_cols[:, QK_ROPE_DIM:])
    k_rope = k_rope.astype(k_ref.dtype)
    k_nope = jnp.dot(ckv, wuk_ref[...], preferred_element_type=F32)

    qt_nope = lax.dot_general(wqn_t_ref[...], qn, nt, preferred_element_type=F32) * Q_SCALE
    qt_ra = lax.dot_general(wqr_t_ref[...], qn, nt, preferred_element_type=F32)
    qt_rb = lax.dot_general(wqrot_t_ref[...], qn, nt, preferred_element_type=F32)
    vt = lax.dot_general(wuv_t_ref[...], ckv, nt, preferred_element_type=F32)

    for h in range(MLA_HEADS):
        nope = slice(h * QK_NOPE_DIM, (h + 1) * QK_NOPE_DIM)
        rope = slice(h * QK_ROPE_DIM, (h + 1) * QK_ROPE_DIM)
        qt_rope = (qt_ra[rope] * cos2_t + qt_rb[rope] * sin2_t) * Q_SCALE
        qt_ref[0, h, 0:QK_NOPE_DIM, :] = qt_nope[nope].astype(qt_ref.dtype)
        qt_ref[0, h, QK_NOPE_DIM:QK_DIM, :] = qt_rope.astype(qt_ref.dtype)
        k_ref[0, h, :, 0:QK_NOPE_DIM] = k_nope[:, nope].astype(k_ref.dtype)
        k_ref[0, h, :, QK_NOPE_DIM:QK_DIM] = k_rope
        vt_ref[0, h, 0:V_HEAD_DIM, :] = vt[h * V_HEAD_DIM:(h + 1) * V_HEAD_DIM].astype(vt_ref.dtype)
        vt_ref[0, h, V_HEAD_DIM:, :] = jnp.ones((ONES_ROWS, vt.shape[1]), vt_ref.dtype)


def _mla_proj(x_bf16, w_in, gq, gkv, w_qn_t, w_qr_t, w_qrot_t, w_uk, w_uv_t, rope_tab, batch, seq, tm=512):
    t, d = x_bf16.shape
    tm = _tile(seq, tm)
    n_s = seq // tm
    row = lambda width: pl.BlockSpec((tm, width), lambda i: (i, 0))
    rows_blk = lambda width: pl.BlockSpec((1, MLA_HEADS, tm, width), lambda i: (i // n_s, 0, i % n_s, 0))
    cols_blk = lambda height: pl.BlockSpec((1, MLA_HEADS, height, tm), lambda i: (i // n_s, 0, 0, i % n_s))
    consts = [w_in, gq, gkv, w_qn_t, w_qr_t, w_qrot_t, w_uk, w_uv_t]
    return pl.pallas_call(
        _mla_proj_kernel,
        grid=(t // tm,),
        in_specs=([row(d)] + [_const_spec(c.shape) for c in consts]
                  + [pl.BlockSpec((2 * ROPE_FREQS, tm), lambda i: (0, i))]),
        out_specs=[cols_blk(QK_DIM), rows_blk(QK_DIM), cols_blk(V_HEAD_DIM + ONES_ROWS), row(MEM_WIDTH)],
        out_shape=[jax.ShapeDtypeStruct((batch, MLA_HEADS, QK_DIM, seq), BF16),
                   jax.ShapeDtypeStruct((batch, MLA_HEADS, seq, QK_DIM), BF16),
                   jax.ShapeDtypeStruct((batch, MLA_HEADS, V_HEAD_DIM + ONES_ROWS, seq), BF16),
                   jax.ShapeDtypeStruct((t, MEM_WIDTH), BF16)],
        compiler_params=_params("parallel"),
        name="mla_proj",
    )(x_bf16, *consts, rope_tab)


def _flash_kernel(qt_ref, qt_next_ref, k_ref, vt_ref, o_ref, s0_ref, s1_ref, cm0_ref, cm1_ref, m_ref, acc_ref,
                  *, tk, n_chunks, unroll):
    s_refs, cm_refs = (s0_ref, s1_ref), (cm0_ref, cm1_ref)
    dv = o_ref.shape[2]

    def scores(j, slot, q_ref=qt_ref):
        off = pl.multiple_of(j * tk, tk)
        s = jnp.dot(k_ref[0, 0, pl.ds(off, tk), :], q_ref[0, 0], preferred_element_type=F32)
        s_refs[slot][...] = s
        cm_refs[slot][...] = jnp.max(s, axis=0, keepdims=True)

    def attend(j, slot):
        off = pl.multiple_of(j * tk, tk)
        m_prev = m_ref[...]
        m_new = jnp.maximum(m_prev, cm_refs[slot][...])
        m_ref[...] = m_new
        p = jnp.exp2(s_refs[slot][...] - m_new).astype(BF16)
        pv = jnp.dot(vt_ref[0, 0, :, pl.ds(off, tk)], p, preferred_element_type=F32)
        acc_ref[...] = jnp.exp2(m_prev - m_new) * acc_ref[...] + pv

    @pl.when(pl.program_id(2) == 0)
    def _():
        scores(0, 0)

    m_ref[...] = jnp.full_like(m_ref, -jnp.inf)
    acc_ref[...] = jnp.zeros_like(acc_ref)

    def two_chunks(i, carry):
        j = 2 * i
        scores(j + 1, 1)
        attend(j, 0)
        scores(j + 2, 0)
        attend(j + 1, 1)
        return carry

    lax.fori_loop(0, n_chunks // 2 - 1, two_chunks, 0, unroll=unroll)
    scores(n_chunks - 1, 1)
    attend(n_chunks - 2, 0)
    scores(0, 0, qt_next_ref)
    attend(n_chunks - 1, 1)
    o_ref[0] = (acc_ref[0:dv, :] / acc_ref[dv:dv + 1, :]).T.astype(o_ref.dtype)


def _flash_attention(qt, k, vt_ones, tq=1024, tk=512, unroll=5):
    batch, heads, dk, seq = qt.shape
    rows = vt_ones.shape[2]
    dv = rows - ONES_ROWS
    tq, tk = _tile(seq, tq), _tile(seq, tk)
    n_chunks, n_q = seq // tk, seq // tq
    assert n_chunks >= 2 and n_chunks % 2 == 0, n_chunks
    n_loop = n_chunks // 2 - 1
    return pl.pallas_call(
        functools.partial(_flash_kernel, tk=tk, n_chunks=n_chunks,
                          unroll=unroll if n_loop % unroll == 0 else 1),
        grid=(batch, heads, n_q),
        in_specs=[pl.BlockSpec((1, 1, dk, tq), lambda b, h, i: (b, h, 0, i)),
                  pl.BlockSpec((1, 1, dk, tq), lambda b, h, i: (b, h, 0, jnp.minimum(i + 1, n_q - 1))),
                  pl.BlockSpec((1, 1, seq, dk), lambda b, h, i: (b, h, 0, 0)),
                  pl.BlockSpec((1, 1, rows, seq), lambda b, h, i: (b, h, 0, 0))],
        out_specs=pl.BlockSpec((1, tq, dv), lambda b, h, i: (b, i, h)),
        out_shape=jax.ShapeDtypeStruct((batch, seq, heads * dv), BF16),
        scratch_shapes=[pltpu.VMEM((tk, tq), F32), pltpu.VMEM((tk, tq), F32),
                        pltpu.VMEM((1, tq), F32), pltpu.VMEM((1, tq), F32), pltpu.VMEM((1, tq), F32),
                        pltpu.VMEM((rows, tq), F32)],
        compiler_params=_params("parallel", "parallel", "arbitrary"),
        name="flash_attention",
    )(qt, qt, k, vt_ones)


def _rotate_half_cols(w):
    half = w.shape[-1] // 2
    return jnp.concatenate([-w[..., half:], w[..., :half]], axis=-1)


def _mla_weights(w_in, w_uq):
    o1 = Q_LORA_RANK
    o2 = o1 + KV_LORA_RANK
    o3 = o2 + QK_ROPE_DIM
    k_rope_w = w_in[:, o2:o3]
    w_in_r = jnp.concatenate([w_in[:, :o2], w_in[:, o3:], k_rope_w, _rotate_half_cols(k_rope_w)], axis=1)
    rank = w_uq.shape[0]
    w3 = w_uq.reshape(rank, MLA_HEADS, QK_DIM)
    w_qn = w3[:, :, :QK_NOPE_DIM].reshape(rank, MLA_HEADS * QK_NOPE_DIM)
    w_qr = w3[:, :, QK_NOPE_DIM:]
    w_qrot = _rotate_half_cols(w_qr).reshape(rank, MLA_HEADS * QK_ROPE_DIM)
    w_qr = w_qr.reshape(rank, MLA_HEADS * QK_ROPE_DIM)
    return w_in_r.astype(BF16), w_qn.T.astype(BF16), w_qr.T.astype(BF16), w_qrot.T.astype(BF16)


def kernel(x, mem, positions, pool_w_in, pool_group_w, pool_scale, pool_w_out, mla_w_in, mla_q_norm_g,
           mla_kv_norm_g, mla_w_uq, mla_w_uk, mla_w_uv, mla_w_out, mem_w_kv, ln1_g, ln1_b, ln2_g, ln2_b,
           ffn_w_up, ffn_conv_w, ffn_conv_b, ffn_w_down):
    batch, seq, d = x.shape
    depth = ln1_g.shape[0]
    alpha = (2 * depth) ** 0.25
    t = batch * seq
    token_width = pool_w_in.shape[-1] - MEM_WIDTH

    xf = x.reshape(t, d)
    xb = None
    mem2 = mem.reshape(batch * N_MEM, d)
    rope_tab = _rope_table(positions) if depth > 1 else None

    for i in range(depth):
        j = i // 2
        kv_mem = _matmul(mem2, mem_w_kv[i].astype(BF16), BF16)
        if i % 2 == 0:
            z = _matmul(xf if xb is None else xb, pool_w_in[j].astype(BF16), BF16,
                        tm=512, tn=pool_w_in.shape[-1])
            y_tok = _pool_mix(z, pool_group_w[j].astype(BF16), pool_scale[j], seq)
            y_mem = _mem_attn(z, token_width // MEM_WIDTH, kv_mem, seq)
            w_out = pool_w_out[j].astype(BF16)
        else:
            w_in_r, w_qn_t, w_qr_t, w_qrot_t = _mla_weights(mla_w_in[j], mla_w_uq[j])
            qt, k, vt, q_mem = _mla_proj(
                xb, w_in_r, mla_q_norm_g[j].reshape(1, -1), mla_kv_norm_g[j].reshape(1, -1),
                w_qn_t, w_qr_t, w_qrot_t, mla_w_uk[j].astype(BF16), mla_w_uv[j].T.astype(BF16),
                rope_tab, batch, seq)
            y_tok = _flash_attention(qt, k, vt).reshape(t, MLA_HEADS * V_HEAD_DIM)
            y_mem = _mem_attn(q_mem, 0, kv_mem, seq)
            w_out = mla_w_out[j].astype(BF16)
        split = y_tok.shape[1]
        xf, xb = _outproj_ln(y_tok, y_mem, w_out[:split], w_out[split:], xf, ln1_g[i], ln1_b[i], alpha)
        xf, xb = _conv_ffn_ln(xf, xb, ffn_w_up[i].astype(BF16), ffn_conv_w[i], ffn_conv_b[i],
                              ffn_w_down[i].astype(BF16), ln2_g[i], ln2_b[i], alpha, seq)
    return xf.reshape(batch, seq, d)
```
